```python
import math
import jax
import jax.numpy as jnp
from jax import lax
import numpy as np

D_MODEL = 1024
BATCH = 8
SEQ = 2048
DEPTH = 2
DEC_BATCH = 4
DEC_SEQ = 4096
PAST_LEN = 128

GRID_W = 64
HEAD_DIM = 64
N_GROUPS = 4
GROUP_W = D_MODEL // N_GROUPS
D_MIX = N_GROUPS * GROUP_W
A_HEADS = GROUP_W // HEAD_DIM
A_DH = HEAD_DIM // 2
A_COLS = 3 * GROUP_W
B_CH = GROUP_W
HY_EMB = 33
HY_BANDS = (HY_EMB - 1) // 2
HY_FFN = 64
HY_INNER = 2
HY_SHORT = 3
HY_TARGET = 1e-2
HY_FAST = 0.3
HY_SLOW = 1.5
B_COLS = 3 * B_CH
C_HEADS = GROUP_W // HEAD_DIM
NA_WR = 8
NA_WC = 16
C_COLS = 3 * GROUP_W
D_HEADS = GROUP_W // HEAD_DIM
D_KV_HEADS = 2
D_GROUP = D_HEADS // D_KV_HEADS
WIN = 128
BLOCK = 128
D_COLS = (D_HEADS + 2 * D_KV_HEADS) * HEAD_DIM
D_IN = A_COLS + B_COLS + C_COLS + D_COLS
N_EXPERTS = 256
TOP_K = 8
D_EXPERT = 256
D_SHARED = 256
ROUTED_SCALE = 2.5
MOE_BLK = 64

ROPE_THETA = 10000.0
LN_EPS = 1e-5
DN_ALPHA = (2 * DEPTH) ** 0.25
DN_BETA = (8 * DEPTH) ** -0.25
NEG = -1e30

kernel_name = 'hybrid_parallel_encoder_moe'


def layer_norm(x, g, b):
    xf = x.astype(jnp.float32)
    mu = jnp.mean(xf, axis=-1, keepdims=True)
    var = jnp.mean(jnp.square(xf - mu), axis=-1, keepdims=True)
    y = (xf - mu) * lax.rsqrt(var + LN_EPS) * g.astype(jnp.float32) + b.astype(jnp.float32)
    return y.astype(x.dtype)


def rope(x, pos):
    dh = x.shape[-1]
    inv_freq = ROPE_THETA ** (-jnp.arange(0, dh, 2, dtype=jnp.float32) / dh)
    ang = pos.astype(jnp.float32)[:, None] * inv_freq[None, :]
    cos = jnp.cos(ang)[:, None, :]
    sin = jnp.sin(ang)[:, None, :]
    xf = x.astype(jnp.float32)
    x1, x2 = xf[..., : dh // 2], xf[..., dh // 2:]
    return jnp.concatenate([x1 * cos - x2 * sin, x2 * cos + x1 * sin], axis=-1).astype(x.dtype)


def diff_attention(q, k, v, lam_q1, lam_k1, lam_q2, lam_k2, subln_g, lam_init):
    bsz, L, _ = q.shape
    pos = jnp.arange(L)
    q = rope(q.reshape(bsz, L, 2 * A_HEADS, A_DH), pos).reshape(bsz, L, A_HEADS, 2, A_DH)
    k = rope(k.reshape(bsz, L, 2 * A_HEADS, A_DH), pos).reshape(bsz, L, A_HEADS, 2, A_DH)
    vf = v.reshape(bsz, L, A_HEADS, HEAD_DIM).astype(jnp.float32)
    lam = (jnp.exp(jnp.sum(lam_q1.astype(jnp.float32) * lam_k1.astype(jnp.float32)))
           - jnp.exp(jnp.sum(lam_q2.astype(jnp.float32) * lam_k2.astype(jnp.float32)))
           + lam_init)
    n_blk = L // BLOCK
    q_blocks = jnp.moveaxis(q.reshape(bsz, n_blk, BLOCK, A_HEADS, 2, A_DH), 1, 0)

    def one_block(qb):
        s = jnp.einsum('bqhcd,bkhcd->bhcqk', qb, k, preferred_element_type=jnp.float32) * (A_DH ** -0.5)
        p = jax.nn.softmax(s, axis=-1)
        w = p[:, :, 0] - lam * p[:, :, 1]
        return jnp.einsum('bhqk,bkhd->bqhd', w, vf)

    o = lax.map(one_block, q_blocks)
    o = jnp.moveaxis(o, 0, 1).reshape(bsz, L, A_HEADS, HEAD_DIM)
    o = o * lax.rsqrt(jnp.mean(jnp.square(o), axis=-1, keepdims=True) + LN_EPS)
    o = o * subln_g.astype(jnp.float32) * (1.0 - lam_init)
    return o.reshape(bsz, L, A_HEADS * HEAD_DIM).astype(q.dtype)


def short_conv(x, w, b):
    L = x.shape[1]
    half = HY_SHORT // 2
    xp = jnp.pad(x, ((0, 0), (half, half), (0, 0)))
    y = b
    for j in range(HY_SHORT):
        y = y + xp[:, j:j + L] * w[j]
    return y


def hyena_filters(L, w1, b1, w2, b2, w3, freq):
    t = jnp.linspace(0.0, 1.0, L, dtype=jnp.float32)[:, None]
    w = 2.0 * math.pi * jnp.arange(L, dtype=jnp.float32)[:, None] / L
    f = jnp.linspace(1e-4, HY_BANDS - 1, HY_BANDS, dtype=jnp.float32)[None, :]
    z = jnp.concatenate([t, jnp.cos(f * w), -jnp.sin(f * w)], axis=-1)
    fr = freq.astype(jnp.float32)
    h = jnp.sin(fr * (z @ w1.astype(jnp.float32) + b1.astype(jnp.float32)))
    for i in range(HY_INNER):
        h = jnp.sin(fr * (h @ w2[i].astype(jnp.float32) + b2[i].astype(jnp.float32)))
    h = h @ w3.astype(jnp.float32)
    max_decay = math.log(HY_TARGET) / HY_FAST
    min_decay = math.log(HY_TARGET) / HY_SLOW
    deltas = jnp.linspace(min_decay, max_decay, B_CH, dtype=jnp.float32)
    decay = jnp.exp(-t * jnp.abs(deltas)[None, :])
    h = h.reshape(L, 2, B_CH) * decay[:, None, :]
    return h[:, 0], h[:, 1]


def centred_long_conv(u, h_f, h_b, bias):
    L, C = u.shape[1], u.shape[2]
    k = jnp.concatenate([h_f, jnp.zeros((1, C), jnp.float32), h_b[1:][::-1]], axis=0)
    kf = jnp.fft.rfft(k, axis=0)
    uf = jnp.fft.rfft(u.astype(jnp.float32), n=2 * L, axis=1)
    y = jnp.fft.irfft(uf * kf[None], n=2 * L, axis=1)[:, :L]
    return (y + u.astype(jnp.float32) * bias.astype(jnp.float32)).astype(u.dtype)


def hyena_mixer(u, conv_w, conv_b, w1, b1, w2, b2, w3, freq, bias):
    L = u.shape[1]
    u = short_conv(u, conv_w, conv_b)
    x0, x1, v = jnp.split(u, 3, axis=-1)
    h_f, h_b = hyena_filters(L, w1, b1, w2, b2, w3, freq)
    v = centred_long_conv(v * x1, h_f, h_b, bias)
    return v * x0


def neighbourhood_attention(q, k, v, rpb):
    bsz, L, _ = q.shape
    rows = L // GRID_W
    wr = min(NA_WR, rows)
    shp = (bsz, rows, GRID_W, C_HEADS, HEAD_DIM)
    q, k, v = q.reshape(shp), k.reshape(shp), v.reshape(shp)
    r = jnp.arange(rows)
    c = jnp.arange(GRID_W)
    key_rows = jnp.clip(r - wr // 2, 0, rows - wr)[:, None] + jnp.arange(wr)[None, :]
    c0 = jnp.clip(c - NA_WC // 2, 0, GRID_W - NA_WC)
    col_ok = (c[None, :] >= c0[:, None]) & (c[None, :] < c0[:, None] + NA_WC)
    k_g = k[:, key_rows]
    v_g = v[:, key_rows].astype(jnp.float32)
    s = jnp.einsum('brqhd,brwkhd->bhrqwk', q, k_g, preferred_element_type=jnp.float32) * (HEAD_DIM ** -0.5)
    dr = key_rows - r[:, None] + (NA_WR - 1)
    dc = jnp.clip(c[None, :] - c[:, None], 1 - NA_WC, NA_WC - 1) + (NA_WC - 1)
    bias = rpb.astype(jnp.float32)[:, dr[:, None, :, None], dc[None, :, None, :]]
    s = jnp.where(col_ok[:, None, :], s + bias[None], NEG)
    p = jax.nn.softmax(s.reshape(bsz, C_HEADS, rows, GRID_W, wr * GRID_W), axis=-1).reshape(s.shape)
    o = jnp.einsum('bhrqwk,brwkhd->brqhd', p, v_g)
    return o.reshape(bsz, L, C_HEADS * HEAD_DIM).astype(q.dtype)


def window_gqa_sink(q, k, v, sink):
    bsz, L, _ = q.shape
    n_blk = L // BLOCK
    pos = jnp.arange(L)
    out_dtype = q.dtype
    q = rope(q.reshape(bsz, L, D_HEADS, HEAD_DIM), pos).reshape(bsz, n_blk, BLOCK, D_KV_HEADS, D_GROUP, HEAD_DIM)
    k = rope(k.reshape(bsz, L, D_KV_HEADS, HEAD_DIM), pos)
    v = v.reshape(bsz, L, D_KV_HEADS, HEAD_DIM)

    def band(t):
        tp = jnp.pad(t, ((0, 0), (BLOCK, BLOCK), (0, 0), (0, 0))).reshape(bsz, n_blk + 2, BLOCK, D_KV_HEADS, HEAD_DIM)
        return jnp.concatenate([tp[:, :-2], tp[:, 1:-1], tp[:, 2:]], axis=2)

    k_b = band(k)
    v_b = band(v).astype(jnp.float32)
    s = jnp.einsum('bnqhgd,bnkhd->bnhgqk', q, k_b, preferred_element_type=jnp.float32) * (HEAD_DIM ** -0.5)
    blk = jnp.arange(n_blk)[:, None]
    q_pos = blk * BLOCK + jnp.arange(BLOCK)[None, :]
    k_pos = (blk - 1) * BLOCK + jnp.arange(3 * BLOCK)[None, :]
    ok = ((jnp.abs(q_pos[:, :, None] - k_pos[:, None, :]) <= WIN)
          & (k_pos[:, None, :] >= 0) & (k_pos[:, None, :] < L))
    s = jnp.where(ok[None, :, None, None], s, NEG)
    sink_logit = jnp.broadcast_to(sink.astype(jnp.float32).reshape(1, 1, D_KV_HEADS, D_GROUP, 1, 1),
                                  s.shape[:-1] + (1,))
    p = jax.nn.softmax(jnp.concatenate([s, sink_logit], axis=-1), axis=-1)[..., :-1]
    o = jnp.einsum('bnhgqk,bnkhd->bnqhgd', p, v_b)
    return o.reshape(bsz, L, D_HEADS * HEAD_DIM).astype(out_dtype)


def moe_ffn(x, router_w, router_bias, e_gate, e_up, e_down, s_gate, s_up, s_down):
    bsz, L, D = x.shape
    n_tok = bsz * L
    n_asg = n_tok * TOP_K
    n_blk = -(-(n_asg + N_EXPERTS * (MOE_BLK - 1)) // MOE_BLK)
    xt = x.reshape(n_tok, D)
    scores = jax.nn.sigmoid(jnp.dot(xt, router_w, preferred_element_type=jnp.float32))
    _, top_e = lax.top_k(scores + router_bias.astype(jnp.float32), TOP_K)
    top_s = jnp.take_along_axis(scores, top_e, axis=-1)
    gates = top_s / jnp.sum(top_s, axis=-1, keepdims=True) * ROUTED_SCALE
    flat_e = top_e.reshape(-1)
    flat_tok = jnp.arange(n_asg, dtype=jnp.int32) // TOP_K
    order = jnp.argsort(flat_e)
    se = flat_e[order]
    counts = jnp.bincount(flat_e, length=N_EXPERTS)
    starts = jnp.cumsum(counts) - counts
    padded = (counts + MOE_BLK - 1) // MOE_BLK * MOE_BLK
    pends = jnp.cumsum(padded)
    dest = (pends - padded)[se] + jnp.arange(n_asg, dtype=jnp.int32) - starts[se]
    row_tok = jnp.full((n_blk * MOE_BLK,), n_tok, jnp.int32).at[dest].set(flat_tok[order])
    row_gate = jnp.zeros((n_blk * MOE_BLK,), jnp.float32).at[dest].set(gates.reshape(-1)[order])
    blk_e = jnp.minimum(jnp.searchsorted(pends, jnp.arange(n_blk) * MOE_BLK, side='right'), N_EXPERTS - 1)
    xpad = jnp.concatenate([xt, jnp.zeros((1, D), xt.dtype)], axis=0)

    def expert_block(args):
        e, tok, g = args
        xb = xpad[tok]
        hdn = jax.nn.silu(xb @ e_gate[e]) * (xb @ e_up[e])
        return (hdn @ e_down[e]).astype(jnp.float32) * g[:, None]

    y_rows = lax.map(expert_block, (blk_e, row_tok.reshape(n_blk, MOE_BLK), row_gate.reshape(n_blk, MOE_BLK)))
    routed = jax.ops.segment_sum(y_rows.reshape(-1, D), row_tok, num_segments=n_tok + 1)[:n_tok]
    shared = (jax.nn.silu(xt @ s_gate) * (xt @ s_up)) @ s_down
    return (routed + shared.astype(jnp.float32)).astype(x.dtype).reshape(bsz, L, D)


def encoder_layer(x, p, lam_init):
    proj = jnp.einsum('bld,de->ble', x, p['w_in'])
    pa, pb, pc, pd = jnp.split(proj, [A_COLS, A_COLS + B_COLS, A_COLS + B_COLS + C_COLS], axis=-1)
    qa, ka, va = jnp.split(pa, 3, axis=-1)
    o_a = diff_attention(qa, ka, va, p['lam_q1'], p['lam_k1'], p['lam_q2'], p['lam_k2'], p['a_subln_g'], lam_init)
    o_b = hyena_mixer(pb, p['hy_conv_w'], p['hy_conv_b'], p['hy_w1'], p['hy_b1'], p['hy_w2'], p['hy_b2'],
                      p['hy_w3'], p['hy_freq'], p['hy_bias'])
    qc, kc, vc = jnp.split(pc, 3, axis=-1)
    o_c = neighbourhood_attention(qc, kc, vc, p['na_rpb'])
    qd, kd, vd = jnp.split(pd, [D_HEADS * HEAD_DIM, (D_HEADS + D_KV_HEADS) * HEAD_DIM], axis=-1)
    o_d = window_gqa_sink(qd, kd, vd, p['d_sink'])
    mixed = jnp.concatenate([o_a, o_b.astype(x.dtype), o_c, o_d], axis=-1)
    mix = jnp.einsum('ble,ed->bld', mixed, p['w_out'])
    x = layer_norm(DN_ALPHA * x + mix, p['ln1_g'], p['ln1_b'])
    ffn = moe_ffn(x, p['router_w'], p['router_bias'], p['e_gate'], p['e_up'], p['e_down'],
                  p['s_gate'], p['s_up'], p['s_down'])
    return layer_norm(DN_ALPHA * x + ffn, p['ln2_g'], p['ln2_b'])


def encoder_trunk(x, params):
    for l in range(DEPTH):
        lp = {name: arr[l] for name, arr in params.items()}
        x = encoder_layer(x, lp, 0.8 - 0.6 * math.exp(-0.3 * l))
    return x


def setup_inputs(seed: int = 0) -> dict:
    key = jax.random.key(seed)
    ks = jax.random.split(key, 40)

    def nrm(k, shape, scale):
        return jax.random.normal(k, shape, jnp.float32) * scale

    return {
        'x_prompt': nrm(ks[0], (BATCH, SEQ, D_MODEL), 1.0),
        'x_sample': nrm(ks[1], (DEC_BATCH, DEC_SEQ, D_MODEL), 1.0),
        'w_in': nrm(ks[2], (DEPTH, D_MODEL, D_IN), D_MODEL ** -0.5),
        'w_out': nrm(ks[3], (DEPTH, D_MIX, D_MODEL), D_MIX ** -0.5 * DN_BETA),
        'ln1_g': 1.0 + nrm(ks[4], (DEPTH, D_MODEL), 0.02),
        'ln1_b': nrm(ks[5], (DEPTH, D_MODEL), 0.02),
        'ln2_g': 1.0 + nrm(ks[6], (DEPTH, D_MODEL), 0.02),
        'ln2_b': nrm(ks[7], (DEPTH, D_MODEL), 0.02),
        'lam_q1': nrm(ks[8], (DEPTH, A_DH), 0.1),
        'lam_k1': nrm(ks[9], (DEPTH, A_DH), 0.1),
        'lam_q2': nrm(ks[10], (DEPTH, A_DH), 0.1),
        'lam_k2': nrm(ks[11], (DEPTH, A_DH), 0.1),
        'a_subln_g': 1.0 + nrm(ks[12], (DEPTH, HEAD_DIM), 0.02),
        'hy_conv_w': nrm(ks[13], (DEPTH, HY_SHORT, 3 * B_CH), HY_SHORT ** -0.5),
        'hy_conv_b': nrm(ks[14], (DEPTH, 3 * B_CH), 0.02),
        'hy_w1': nrm(ks[15], (DEPTH, HY_EMB, HY_FFN), HY_EMB ** -0.5),
        'hy_b1': nrm(ks[16], (DEPTH, HY_FFN), 0.02),
        'hy_w2': nrm(ks[17], (DEPTH, HY_INNER, HY_FFN, HY_FFN), HY_FFN ** -0.5),
        'hy_b2': nrm(ks[18], (DEPTH, HY_INNER, HY_FFN), 0.02),
        'hy_w3': nrm(ks[19], (DEPTH, HY_FFN, 2 * B_CH), 0.05 * HY_FFN ** -0.5),
        'hy_freq': 1.0 + nrm(ks[20], (DEPTH, HY_FFN), 0.02),
        'hy_bias': nrm(ks[21], (DEPTH, B_CH), 0.1),
        'na_rpb': nrm(ks[22], (DEPTH, C_HEADS, 2 * NA_WR - 1, 2 * NA_WC - 1), 0.02),
        'd_sink': nrm(ks[23], (DEPTH, D_HEADS), 0.1),
        'router_w': nrm(ks[24], (DEPTH, D_MODEL, N_EXPERTS), D_MODEL ** -0.5),
        'router_bias': nrm(ks[25], (DEPTH, N_EXPERTS), 0.01),
        'e_gate': nrm(ks[26], (DEPTH, N_EXPERTS, D_MODEL, D_EXPERT), D_MODEL ** -0.5),
        'e_up': nrm(ks[27], (DEPTH, N_EXPERTS, D_MODEL, D_EXPERT), D_MODEL ** -0.5),
        'e_down': nrm(ks[28], (DEPTH, N_EXPERTS, D_EXPERT, D_MODEL), D_EXPERT ** -0.5 * DN_BETA),
        's_gate': nrm(ks[29], (DEPTH, D_MODEL, D_SHARED), D_MODEL ** -0.5),
        's_up': nrm(ks[30], (DEPTH, D_MODEL, D_SHARED), D_MODEL ** -0.5),
        's_down': nrm(ks[31], (DEPTH, D_SHARED, D_MODEL), D_SHARED ** -0.5 * DN_BETA),
    }


def reference(x_prompt, x_sample, w_in, w_out, ln1_g, ln1_b, ln2_g, ln2_b,
              lam_q1, lam_k1, lam_q2, lam_k2, a_subln_g,
              hy_conv_w, hy_conv_b, hy_w1, hy_b1, hy_w2, hy_b2, hy_w3, hy_freq, hy_bias,
              na_rpb, d_sink, router_w, router_bias, e_gate, e_up, e_down, s_gate, s_up, s_down):
    params = {
        'w_in': w_in, 'w_out': w_out, 'ln1_g': ln1_g, 'ln1_b': ln1_b, 'ln2_g': ln2_g, 'ln2_b': ln2_b,
        'lam_q1': lam_q1, 'lam_k1': lam_k1, 'lam_q2': lam_q2, 'lam_k2': lam_k2, 'a_subln_g': a_subln_g,
        'hy_conv_w': hy_conv_w, 'hy_conv_b': hy_conv_b, 'hy_w1': hy_w1, 'hy_b1': hy_b1,
        'hy_w2': hy_w2, 'hy_b2': hy_b2, 'hy_w3': hy_w3, 'hy_freq': hy_freq, 'hy_bias': hy_bias,
        'na_rpb': na_rpb, 'd_sink': d_sink, 'router_w': router_w, 'router_bias': router_bias,
        'e_gate': e_gate, 'e_up': e_up, 'e_down': e_down, 's_gate': s_gate, 's_up': s_up, 's_down': s_down,
    }
    y_prompt = encoder_trunk(x_prompt, params)
    y_sample = encoder_trunk(x_sample, params)
    return (y_prompt, y_sample)
```

```python
import functools
import math

import jax
import jax.numpy as jnp
from jax import lax
from jax.experimental import pallas as pl
from jax.experimental.pallas import tpu as pltpu

F32 = jnp.float32
BF16 = jnp.bfloat16
I32 = jnp.int32

D_MODEL = 1024
DEPTH = 2
GRID_W = 64
HEAD_DIM = 64
GROUP_W = 256
A_HEADS = 4
A_DH = 32
B_CH = 256
HY_EMB = 33
HY_BANDS = 16
HY_FFN = 64
HY_INNER = 2
HY_TARGET = 1e-2
HY_FAST = 0.3
HY_SLOW = 1.5
C_HEADS = 4
NA_WR = 8
NA_WC = 16
D_HEADS = 4
D_KV_HEADS = 2
WIN = 128
D_IN = 2816
N_EXPERTS = 256
TOP_K = 8
D_EXPERT = 256
D_SHARED = 256
ROUTED_SCALE = 2.5
ROPE_THETA = 10000.0
LN_EPS = 1e-5
DN_ALPHA = (2 * DEPTH) ** 0.25
NEG = -1e30
LOG2E = 1.4426950408889634

LANES = 128
VMEM_LIMIT = 48 * 1024 * 1024

COL_AQ, COL_AK, COL_AV = 0, 1, 2
COL_B = 1
COL_CQ, COL_CK, COL_CV = 6, 7, 8
COL_DQ = 9
COL_DK, COL_DV = 20, 21

EXP_BLK = 128
CMB_TOK = 64


def _cparams(sem):
    return pltpu.CompilerParams(dimension_semantics=sem, vmem_limit_bytes=VMEM_LIMIT)


def _rope_lanes(y, cos, sin_signed, half):
    lane = lax.broadcasted_iota(I32, y.shape, 1)
    first = (lane % (2 * half)) < half
    rot = jnp.where(first, pltpu.roll(y, LANES - half, 1), pltpu.roll(y, half, 1))
    return y * cos + rot * sin_signed


def _in_proj_kernel(x_ref, w_ref, ca_ref, sa_ref, cd_ref, sd_ref, o_ref):
    xb = x_ref[...].astype(BF16)
    ca, sa, cd, sd = ca_ref[...], sa_ref[...], cd_ref[...], sd_ref[...]
    a_scale = (A_DH ** -0.5) * LOG2E
    hd_scale = (HEAD_DIM ** -0.5) * LOG2E
    for c in range(D_IN // 256):
        y = jnp.dot(xb, w_ref[:, c * 256:(c + 1) * 256], preferred_element_type=F32)
        for hh in range(2):
            ch = 2 * c + hh
            z = y[:, hh * LANES:(hh + 1) * LANES]
            if ch in (0, 1):
                z = _rope_lanes(z, ca, sa, A_DH // 2) * a_scale
            elif ch in (2, 3):
                z = _rope_lanes(z, ca, sa, A_DH // 2)
            elif ch in (12, 13):
                z = z * hd_scale
            elif ch in (18, 19):
                z = _rope_lanes(z, cd, sd, HEAD_DIM // 2) * hd_scale
            elif ch == 20:
                z = _rope_lanes(z, cd, sd, HEAD_DIM // 2)
            o_ref[:, ch * LANES:(ch + 1) * LANES] = z.astype(BF16)


def _in_proj(x, w_bf, ca, sa, cd, sd, L, tm=512):
    T = x.shape[0]
    nl = L // tm
    tab = pl.BlockSpec((tm, LANES), lambda i: (i % nl, 0))
    return pl.pallas_call(
        _in_proj_kernel,
        out_shape=jax.ShapeDtypeStruct((T, D_IN), BF16),
        grid=(T // tm,),
        in_specs=[pl.BlockSpec((tm, D_MODEL), lambda i: (i, 0)),
                  pl.BlockSpec((D_MODEL, D_IN), lambda i: (0, 0)),
                  tab, tab, tab, tab],
        out_specs=pl.BlockSpec((tm, D_IN), lambda i: (i, 0)),
        compiler_params=_cparams(("parallel",)),
        name="in_proj",
    )(x, w_bf, ca, sa, cd, sd)


def _rope_tables(L, dh):
    half = dh // 2
    lane = jnp.arange(LANES)
    inv_freq = ROPE_THETA ** (-(2.0 * (lane % half)).astype(F32) / dh)
    ang = jnp.arange(L, dtype=F32)[:, None] * inv_freq[None, :]
    sign = jnp.where((lane % dh) < half, -1.0, 1.0).astype(F32)
    return jnp.cos(ang), jnp.sin(ang) * sign[None, :]


def _diff_attn_kernel(lam_ref, q_ref, k_ref, v_ref, gn_ref, gm_ref, o_ref):
    lam = lam_ref[0]
    tq = q_ref.shape[0]
    q = q_ref[...]
    v = v_ref[...]
    lane = lax.broadcasted_iota(I32, (tq, LANES), 1)
    lane_o = lax.broadcasted_iota(I32, (tq, GROUP_W), 1)
    acc = jnp.zeros((tq, GROUP_W), F32)
    for h in range(A_HEADS):
        parts = []
        for c in range(2):
            comp = 2 * h + c
            j, off = comp // 4, (comp % 4) * A_DH
            qc = q[:, j * LANES:(j + 1) * LANES]
            qm = jnp.where((lane >= off) & (lane < off + A_DH), qc, jnp.zeros_like(qc))
            s = lax.dot_general(qm, k_ref[:, j * LANES:(j + 1) * LANES],
                                (((1,), (1,)), ((), ())), preferred_element_type=F32)
            m = jnp.max(s, axis=-1, keepdims=True)
            e = jnp.exp2(s - m)
            l = jnp.sum(e, axis=-1, keepdims=True)
            pv = jnp.dot(e.astype(BF16), v, preferred_element_type=F32)
            parts.append(pv * (1.0 / l))
        o_h = parts[0] - lam * parts[1]
        acc = jnp.where((lane_o >= h * HEAD_DIM) & (lane_o < (h + 1) * HEAD_DIM), o_h, acc)
    sq = acc * acc
    hi = sq.astype(BF16)
    lo = (sq - hi.astype(F32)).astype(BF16)
    ms = (jnp.dot(hi, gm_ref[...], preferred_element_type=F32)
          + jnp.dot(lo, gm_ref[...], preferred_element_type=F32))
    o_ref[...] = (acc * lax.rsqrt(ms + LN_EPS) * gn_ref[...]).astype(BF16)


def _diff_attn(proj, lam, gn, gmat, B, L, tq=256):
    nq = L // tq
    return pl.pallas_call(
        _diff_attn_kernel,
        out_shape=jax.ShapeDtypeStruct((B * L, GROUP_W), BF16),
        grid=(B, nq),
        in_specs=[pl.BlockSpec(memory_space=pltpu.SMEM),
                  pl.BlockSpec((tq, GROUP_W), lambda b, i: (b * nq + i, COL_AQ)),
                  pl.BlockSpec((L, GROUP_W), lambda b, i: (b, COL_AK)),
                  pl.BlockSpec((L, GROUP_W), lambda b, i: (b, COL_AV)),
                  pl.BlockSpec((1, GROUP_W), lambda b, i: (0, 0)),
                  pl.BlockSpec((GROUP_W, GROUP_W), lambda b, i: (0, 0))],
        out_specs=pl.BlockSpec((tq, GROUP_W), lambda b, i: (b * nq + i, 0)),
        compiler_params=_cparams(("parallel", "parallel")),
        name="diff_attn",
    )(lam, proj, proj, proj, gn, gmat)


HY_RC = 512


def _hy_prep_kernel(p_ref, w_ref, b_ref, u_ref, x0_ref):
    L = p_ref.shape[0]
    w0, w1, w2, bias = w_ref[0:1, :], w_ref[1:2, :], w_ref[2:3, :], b_ref[...]
    row = lax.broadcasted_iota(I32, (HY_RC, 3 * B_CH), 0)
    zero_row = jnp.zeros((1, 3 * B_CH), F32)
    for r in range(L // HY_RC):
        r0 = r * HY_RC
        xf = p_ref[r0:r0 + HY_RC, :].astype(F32)
        prev_row = zero_row if r == 0 else p_ref[r0 - 8:r0, :].astype(F32)[7:8, :]
        next_row = (zero_row if r0 + HY_RC == L
                    else p_ref[r0 + HY_RC:r0 + HY_RC + 8, :].astype(F32)[0:1, :])
        xp = jnp.where(row == 0, prev_row, pltpu.roll(xf, 1, 0))
        xn = jnp.where(row == HY_RC - 1, next_row, pltpu.roll(xf, HY_RC - 1, 0))
        y = bias + xp * w0 + xf * w1 + xn * w2
        x0, x1, v = y[:, :B_CH], y[:, B_CH:2 * B_CH], y[:, 2 * B_CH:]
        u_ref[r0:r0 + HY_RC, :] = (v * x1).astype(BF16)
        x0_ref[r0:r0 + HY_RC, :] = x0.astype(BF16)


def _hy_prep(proj, conv_w, conv_b, B, L):
    out = jax.ShapeDtypeStruct((B * L, B_CH), BF16)
    return pl.pallas_call(
        _hy_prep_kernel,
        out_shape=(out, out),
        grid=(B,),
        in_specs=[pl.BlockSpec((L, 3 * B_CH), lambda b: (b, COL_B)),
                  pl.BlockSpec((3, 3 * B_CH), lambda b: (0, 0)),
                  pl.BlockSpec((1, 3 * B_CH), lambda b: (0, 0))],
        out_specs=(pl.BlockSpec((L, B_CH), lambda b: (b, 0)),
                   pl.BlockSpec((L, B_CH), lambda b: (b, 0))),
        compiler_params=_cparams(("parallel",)),
        name="hy_prep",
    )(proj, conv_w, conv_b)


def _trig_table_kernel(ar_ref, ai_ref, br_ref, bi_ref, c_ref, s_ref):
    br, bi = br_ref[...], bi_ref[...]
    for th in range(c_ref.shape[1] // LANES):
        ar, ai = ar_ref[:, th:th + 1], ai_ref[:, th:th + 1]
        c_ref[:, th * LANES:(th + 1) * LANES] = (ar * br - ai * bi).astype(BF16)
        s_ref[:, th * LANES:(th + 1) * LANES] = (ar * bi + ai * br).astype(BF16)


def _trig_tables(L, transposed, tr=512):
    r = jnp.arange(L, dtype=I32)[:, None]
    hi = jnp.arange(L // LANES, dtype=I32)[None, :]
    lo = jnp.arange(LANES, dtype=I32)[None, :]
    if transposed:
        pa, pb = (2 * LANES * hi * r) % (4 * L), ((2 * lo + 1) * r) % (4 * L)
    else:
        pa, pb = ((2 * r + 1) * LANES * hi) % (4 * L), ((2 * r + 1) * lo) % (4 * L)
    ang_a = pa.astype(F32) * (math.pi / (2 * L))
    ang_b = pb.astype(F32) * (math.pi / (2 * L))
    out = jax.ShapeDtypeStruct((L, L), BF16)
    sa = pl.BlockSpec((tr, L // LANES), lambda i: (i, 0))
    sb = pl.BlockSpec((tr, LANES), lambda i: (i, 0))
    so = pl.BlockSpec((tr, L), lambda i: (i, 0))
    return pl.pallas_call(
        _trig_table_kernel, out_shape=(out, out), grid=(L // tr,),
        in_specs=[sa, sa, sb, sb], out_specs=(so, so),
        compiler_params=_cparams(("parallel",)), name="trig_tables",
    )(jnp.cos(ang_a), jnp.sin(ang_a), jnp.cos(ang_b), jnp.sin(ang_b))


def _hy_filter_kernel(z_ref, t_ref, m_ref, w1_ref, b1_ref, w2_ref, b2_ref, w3_ref, fr_ref,
                      ad_ref, k_ref):
    hp = lax.Precision.HIGHEST
    fr = fr_ref[...]
    h = jnp.sin(fr * (jnp.dot(z_ref[0], w1_ref[...], precision=hp, preferred_element_type=F32)
                      + b1_ref[...]))
    for i in range(HY_INNER):
        h = jnp.sin(fr * (jnp.dot(h, w2_ref[i], precision=hp, preferred_element_type=F32)
                          + b2_ref[i]))
    h = jnp.dot(h, w3_ref[...], precision=hp, preferred_element_type=F32)
    decay = jnp.exp(-t_ref[0] * ad_ref[...])
    half = pl.program_id(0)
    sel = jnp.where(half == 0, h[:, :B_CH], -h[:, B_CH:])
    k_ref[...] = (sel * decay * m_ref[0]).astype(BF16)


def _hy_filter(zz, tt, mm, w1p, b1, w2, b2, w3, freq, absdelta, L):
    full = lambda shape: pl.BlockSpec(shape, lambda h: (0,) * len(shape))
    return pl.pallas_call(
        _hy_filter_kernel,
        out_shape=jax.ShapeDtypeStruct((L, 2 * B_CH), BF16),
        grid=(2,),
        in_specs=[pl.BlockSpec((1, L, LANES), lambda h: (h, 0, 0)),
                  pl.BlockSpec((1, L, 1), lambda h: (h, 0, 0)),
                  pl.BlockSpec((1, L, 1), lambda h: (h, 0, 0)),
                  full((LANES, HY_FFN)), full((1, HY_FFN)),
                  full((HY_INNER, HY_FFN, HY_FFN)), full((HY_INNER, 1, HY_FFN)),
                  full((HY_FFN, 2 * B_CH)), full((1, HY_FFN)), full((1, B_CH))],
        out_specs=pl.BlockSpec((L, B_CH), lambda h: (0, h)),
        compiler_params=_cparams(("parallel",)),
        name="hy_filter",
    )(zz, tt, mm, w1p, b1, w2, b2, w3, freq, absdelta)


def _dft_fwd_kernel(c_ref, s_ref, x_ref, oc_ref, os_ref):
    x = x_ref[...]
    oc_ref[...] = jnp.dot(c_ref[...], x, preferred_element_type=F32)
    os_ref[...] = jnp.dot(s_ref[...], x, preferred_element_type=F32)


def _dft_fwd(ctab, stab, x, L, tf=512):
    n = x.shape[1]
    out = jax.ShapeDtypeStruct((L, n), F32)
    return pl.pallas_call(
        _dft_fwd_kernel, out_shape=(out, out), grid=(L // tf,),
        in_specs=[pl.BlockSpec((tf, L), lambda i: (i, 0)),
                  pl.BlockSpec((tf, L), lambda i: (i, 0)),
                  pl.BlockSpec((L, n), lambda i: (0, 0))],
        out_specs=(pl.BlockSpec((tf, n), lambda i: (i, 0)),
                   pl.BlockSpec((tf, n), lambda i: (i, 0))),
        compiler_params=_cparams(("parallel",)), name="dft_filter",
    )(ctab, stab, x)


def _dft_prod_kernel(c_ref, s_ref, u_ref, hc_ref, hs_ref, yr_ref, yi_ref):
    u = u_ref[...]
    uc = jnp.dot(c_ref[...], u, preferred_element_type=F32)
    us = jnp.dot(s_ref[...], u, preferred_element_type=F32)
    tf = uc.shape[0]
    par = lax.broadcasted_iota(I32, (tf, B_CH), 0) % 2
    sgn = jnp.where(par == 0, 1.0, -1.0).astype(F32)
    hc, hs = hc_ref[...], hs_ref[...]
    kc = hc[:, :B_CH] - sgn * hs[:, B_CH:]
    ks = hs[:, :B_CH] + sgn * hc[:, B_CH:]
    yr_ref[...] = (uc * kc - us * ks).astype(BF16)
    yi_ref[...] = (uc * ks + us * kc).astype(BF16)


def _dft_prod(ctab, stab, u, hc, hs, B, L, tf=512):
    nf = L // tf
    out = jax.ShapeDtypeStruct((B * L, B_CH), BF16)
    return pl.pallas_call(
        _dft_prod_kernel, out_shape=(out, out), grid=(nf, B),
        in_specs=[pl.BlockSpec((tf, L), lambda f, b: (f, 0)),
                  pl.BlockSpec((tf, L), lambda f, b: (f, 0)),
                  pl.BlockSpec((L, B_CH), lambda f, b: (b, 0)),
                  pl.BlockSpec((tf, 2 * B_CH), lambda f, b: (f, 0)),
                  pl.BlockSpec((tf, 2 * B_CH), lambda f, b: (f, 0))],
        out_specs=(pl.BlockSpec((tf, B_CH), lambda f, b: (b * nf + f, 0)),
                   pl.BlockSpec((tf, B_CH), lambda f, b: (b * nf + f, 0))),
        compiler_params=_cparams(("parallel", "parallel")), name="dft_fwd_prod",
    )(ctab, stab, u, hc, hs)


def _dft_inv_kernel(ct_ref, st_ref, yr_ref, yi_ref, u_ref, x0_ref, bias_ref, o_ref, *, inv_l):
    y = (jnp.dot(ct_ref[...], yr_ref[...], preferred_element_type=F32)
         + jnp.dot(st_ref[...], yi_ref[...], preferred_element_type=F32)) * inv_l
    v = y + u_ref[...].astype(F32) * bias_ref[...]
    o_ref[...] = (v * x0_ref[...].astype(F32)).astype(BF16)


def _dft_inv(ctt, stt, yr, yi, u, x0, bias, B, L, tt=512):
    nt = L // tt
    return pl.pallas_call(
        functools.partial(_dft_inv_kernel, inv_l=1.0 / L),
        out_shape=jax.ShapeDtypeStruct((B * L, B_CH), BF16), grid=(nt, B),
        in_specs=[pl.BlockSpec((tt, L), lambda t, b: (t, 0)),
                  pl.BlockSpec((tt, L), lambda t, b: (t, 0)),
                  pl.BlockSpec((L, B_CH), lambda t, b: (b, 0)),
                  pl.BlockSpec((L, B_CH), lambda t, b: (b, 0)),
                  pl.BlockSpec((tt, B_CH), lambda t, b: (b * nt + t, 0)),
                  pl.BlockSpec((tt, B_CH), lambda t, b: (b * nt + t, 0)),
                  pl.BlockSpec((1, B_CH), lambda t, b: (0, 0))],
        out_specs=pl.BlockSpec((tt, B_CH), lambda t, b: (b * nt + t, 0)),
        compiler_params=_cparams(("parallel", "parallel")), name="dft_inv",
    )(ctt, stt, yr, yi, u, x0, bias)


def _hy_positions(L):
    pos = jnp.concatenate([jnp.arange(L), L - jnp.arange(L)]) % L
    t = jnp.linspace(0.0, 1.0, L, dtype=F32)[:, None]
    w = 2.0 * math.pi * jnp.arange(L, dtype=F32)[:, None] / L
    f = jnp.linspace(1e-4, HY_BANDS - 1, HY_BANDS, dtype=F32)[None, :]
    z = jnp.concatenate([t, jnp.cos(f * w), -jnp.sin(f * w)], axis=-1)
    zz = jnp.pad(z[pos], ((0, 0), (0, LANES - HY_EMB))).reshape(2, L, LANES)
    tt = t[pos].reshape(2, L, 1)
    mm = jnp.ones((2, L, 1), F32).at[1, 0, 0].set(0.0)
    return zz, tt, mm


def _hyena(proj, p, tabs, B, L):
    ctab, stab, ctt, stt, zz, tt, mm = tabs
    u, x0 = _hy_prep(proj, p['hy_conv_w'], p['hy_conv_b'][None, :], B, L)
    max_decay = math.log(HY_TARGET) / HY_FAST
    min_decay = math.log(HY_TARGET) / HY_SLOW
    absdelta = jnp.abs(jnp.linspace(min_decay, max_decay, B_CH, dtype=F32))[None, :]
    w1p = jnp.pad(p['hy_w1'], ((0, LANES - HY_EMB), (0, 0)))
    k2 = _hy_filter(zz, tt, mm, w1p, p['hy_b1'][None, :], p['hy_w2'], p['hy_b2'][:, None, :],
                    p['hy_w3'], p['hy_freq'][None, :], absdelta, L)
    hc, hs = _dft_fwd(ctab, stab, k2, L)
    yr, yi = _dft_prod(ctab, stab, u, hc, hs, B, L)
    return _dft_inv(ctt, stt, yr, yi, u, x0, p['hy_bias'][None, :], B, L)


def _na_kernel(var_ref, q_ref, k_ref, v_ref, bias_ref, o_ref, *, rows):
    r = pl.program_id(1)
    wr = min(NA_WR, rows)
    kr0 = jnp.clip(r - wr // 2, 0, rows - wr)
    start = pl.multiple_of(kr0 * GRID_W, GRID_W)
    nk = wr * GRID_W
    q = q_ref[...]
    k = k_ref[pl.ds(start, nk), :]
    v = v_ref[pl.ds(start, nk), :]
    lane = lax.broadcasted_iota(I32, (GRID_W, LANES), 1)
    lane_o = lax.broadcasted_iota(I32, (GRID_W, GROUP_W), 1)
    qi = lax.broadcasted_iota(I32, (GRID_W, nk), 0)
    kc = lax.broadcasted_iota(I32, (GRID_W, nk), 1) % GRID_W
    c0 = jnp.clip(qi - NA_WC // 2, 0, GRID_W - NA_WC)
    ok = (kc >= c0) & (kc < c0 + NA_WC)
    acc = jnp.zeros((GRID_W, GROUP_W), F32)
    for h in range(C_HEADS):
        j, off = h // 2, (h % 2) * HEAD_DIM
        qc = q[:, j * LANES:(j + 1) * LANES]
        qm = jnp.where((lane >= off) & (lane < off + HEAD_DIM), qc, jnp.zeros_like(qc))
        s = lax.dot_general(qm, k[:, j * LANES:(j + 1) * LANES], (((1,), (1,)), ((), ())),
                            preferred_element_type=F32)
        s = jnp.where(ok, s + bias_ref[0, h], NEG)
        m = jnp.max(s, axis=-1, keepdims=True)
        e = jnp.exp2(s - m)
        l = jnp.sum(e, axis=-1, keepdims=True)
        pv = jnp.dot(e.astype(BF16), v, preferred_element_type=F32) * (1.0 / l)
        acc = jnp.where((lane_o >= h * HEAD_DIM) & (lane_o < (h + 1) * HEAD_DIM), pv, acc)
    o_ref[...] = acc.astype(BF16)


def _na_bias(rpb, rows):
    wr = min(NA_WR, rows)
    d = -jnp.arange(NA_WR)
    dr = d[:, None] + jnp.arange(wr)[None, :] + (NA_WR - 1)
    c = jnp.arange(GRID_W)
    dc = jnp.clip(c[None, :] - c[:, None], 1 - NA_WC, NA_WC - 1) + (NA_WC - 1)
    bias = rpb.astype(F32)[:, dr[:, None, :, None], dc[None, :, None, :]]
    bias = jnp.moveaxis(bias, 1, 0).reshape(NA_WR, C_HEADS, GRID_W, wr * GRID_W)
    return bias * LOG2E


def _na_attn(proj, bias, B, L):
    rows = L // GRID_W
    wr = min(NA_WR, rows)
    r = jnp.arange(rows)
    variant = (r - jnp.clip(r - wr // 2, 0, rows - wr)).astype(I32)
    nk = wr * GRID_W
    grid_spec = pltpu.PrefetchScalarGridSpec(
        num_scalar_prefetch=1, grid=(B, rows),
        in_specs=[pl.BlockSpec((GRID_W, GROUP_W), lambda b, r, var: (b * rows + r, COL_CQ)),
                  pl.BlockSpec((L, GROUP_W), lambda b, r, var: (b, COL_CK)),
                  pl.BlockSpec((L, GROUP_W), lambda b, r, var: (b, COL_CV)),
                  pl.BlockSpec((1, C_HEADS, GRID_W, nk), lambda b, r, var: (var[r], 0, 0, 0))],
        out_specs=pl.BlockSpec((GRID_W, GROUP_W), lambda b, r, var: (b * rows + r, 0)))
    return pl.pallas_call(
        functools.partial(_na_kernel, rows=rows),
        out_shape=jax.ShapeDtypeStruct((B * L, GROUP_W), BF16),
        grid_spec=grid_spec,
        compiler_params=_cparams(("parallel", "arbitrary")),
        name="na_attn",
    )(variant, proj, proj, proj, bias)


WQ_TILE = 256
WK_SPAN = WQ_TILE + 2 * WIN


def _win_kernel(sink_ref, q_ref, k_ref, v_ref, o_ref, *, L):
    i = pl.program_id(1)
    q0 = i * WQ_TILE
    k0 = pl.multiple_of(jnp.clip(q0 - WIN, 0, L - WK_SPAN), WIN)
    k = k_ref[pl.ds(k0, WK_SPAN), :]
    v = v_ref[pl.ds(k0, WK_SPAN), :]
    qf = q_ref[...].astype(F32)
    qpos = q0 + lax.broadcasted_iota(I32, (WQ_TILE, WK_SPAN), 0)
    kpos = k0 + lax.broadcasted_iota(I32, (WQ_TILE, WK_SPAN), 1)
    ok = jnp.abs(qpos - kpos) <= WIN
    lane = lax.broadcasted_iota(I32, (WQ_TILE, LANES), 1)
    for j in range(2):
        out = jnp.zeros((WQ_TILE, LANES), F32)
        for hh in range(2):
            h = 2 * j + hh
            g = h // (D_HEADS // D_KV_HEADS)
            qc = qf[:, j * LANES:(j + 1) * LANES]
            if hh != g:
                qc = pltpu.roll(qc, HEAD_DIM, 1)
            in_g = (lane >= g * HEAD_DIM) & (lane < (g + 1) * HEAD_DIM)
            qm = jnp.where(in_g, qc, 0.0).astype(BF16)
            s = lax.dot_general(qm, k, (((1,), (1,)), ((), ())), preferred_element_type=F32)
            s = jnp.where(ok, s, NEG)
            sk = sink_ref[h] * LOG2E
            m = jnp.maximum(jnp.max(s, axis=-1, keepdims=True), sk)
            e = jnp.exp2(s - m)
            l = jnp.sum(e, axis=-1, keepdims=True) + jnp.exp2(sk - m)
            pv = jnp.dot(e.astype(BF16), v, preferred_element_type=F32) * (1.0 / l)
            pv = jnp.where(in_g, pv, 0.0)
            if hh != g:
                pv = pltpu.roll(pv, HEAD_DIM, 1)
            out = out + pv
        o_ref[:, j * LANES:(j + 1) * LANES] = out.astype(BF16)


def _win_attn(proj, sink, B, L):
    nq = L // WQ_TILE
    return pl.pallas_call(
        functools.partial(_win_kernel, L=L),
        out_shape=jax.ShapeDtypeStruct((B * L, GROUP_W), BF16),
        grid=(B, nq),
        in_specs=[pl.BlockSpec(memory_space=pltpu.SMEM),
                  pl.BlockSpec((WQ_TILE, GROUP_W), lambda b, i: (b * nq + i, COL_DQ)),
                  pl.BlockSpec((L, LANES), lambda b, i: (b, COL_DK)),
                  pl.BlockSpec((L, LANES), lambda b, i: (b, COL_DV))],
        out_specs=pl.BlockSpec((WQ_TILE, GROUP_W), lambda b, i: (b * nq + i, 0)),
        compiler_params=_cparams(("parallel", "parallel")),
        name="win_attn",
    )(sink, proj, proj, proj)


def _layer_norm(y, g, b):
    mu = jnp.mean(y, axis=-1, keepdims=True)
    d = y - mu
    var = jnp.mean(d * d, axis=-1, keepdims=True)
    return d * lax.rsqrt(var + LN_EPS) * g + b


def _out_proj_kernel(oa_ref, ob_ref, oc_ref, od_ref, w_ref, x_ref, g_ref, b_ref, o_ref):
    acc = DN_ALPHA * x_ref[...]
    for gi, ref in enumerate((oa_ref, ob_ref, oc_ref, od_ref)):
        acc = acc + jnp.dot(ref[...], w_ref[gi * GROUP_W:(gi + 1) * GROUP_W, :],
                            preferred_element_type=F32)
    o_ref[...] = _layer_norm(acc, g_ref[...], b_ref[...])


def _out_proj(oa, ob, oc, od, w_bf, x, g, b, tm=512):
    T = x.shape[0]
    og = pl.BlockSpec((tm, GROUP_W), lambda i: (i, 0))
    row = pl.BlockSpec((tm, D_MODEL), lambda i: (i, 0))
    vec = pl.BlockSpec((1, D_MODEL), lambda i: (0, 0))
    return pl.pallas_call(
        _out_proj_kernel,
        out_shape=jax.ShapeDtypeStruct((T, D_MODEL), F32),
        grid=(T // tm,),
        in_specs=[og, og, og, og, pl.BlockSpec((D_MODEL, D_MODEL), lambda i: (0, 0)), row, vec, vec],
        out_specs=row,
        compiler_params=_cparams(("parallel",)),
        name="out_proj_ln",
    )(oa, ob, oc, od, w_bf, x, g, b)


RT_TM = 256


def _router_kernel(x_ref, wh_ref, wl_ref, bias_ref, tri_ref, te_ref, gate_ref, rank_ref, cnt_ref,
                   carry):
    @pl.when(pl.program_id(0) == 0)
    def _():
        carry[...] = jnp.zeros_like(carry)

    x = x_ref[...]
    xh = x.astype(BF16)
    xl = (x - xh.astype(F32)).astype(BF16)
    logits = (jnp.dot(xh, wh_ref[...], preferred_element_type=F32)
              + jnp.dot(xh, wl_ref[...], preferred_element_type=F32)
              + jnp.dot(xl, wh_ref[...], preferred_element_type=F32))
    scores = 1.0 / (1.0 + jnp.exp(-logits))
    sel = scores + bias_ref[...]
    lane_e = lax.broadcasted_iota(I32, (RT_TM, N_EXPERTS), 1).astype(F32)
    lane_k = lax.broadcasted_iota(I32, (RT_TM, LANES), 1)
    te = jnp.zeros((RT_TM, LANES), F32)
    ts = jnp.zeros((RT_TM, LANES), F32)
    onehot = jnp.zeros((RT_TM, N_EXPERTS), F32)
    for k in range(TOP_K):
        m = jnp.max(sel, axis=-1, keepdims=True)
        idx = jnp.min(jnp.where(sel == m, lane_e, float(N_EXPERTS)), axis=-1, keepdims=True)
        hit = lane_e == idx
        sc = jnp.sum(jnp.where(hit, scores, 0.0), axis=-1, keepdims=True)
        te = jnp.where(lane_k == k, idx, te)
        ts = jnp.where(lane_k == k, sc, ts)
        onehot = jnp.where(hit, 1.0, onehot)
        sel = jnp.where(hit, -jnp.inf, sel)
    gate_ref[...] = ts / jnp.sum(ts, axis=-1, keepdims=True) * ROUTED_SCALE
    te_ref[...] = te.astype(I32)
    before = jnp.dot(tri_ref[...], onehot.astype(BF16), preferred_element_type=F32) + carry[...]
    rank = jnp.zeros((RT_TM, LANES), F32)
    for k in range(TOP_K):
        rk = jnp.sum(jnp.where(lane_e == te[:, k:k + 1], before, 0.0), axis=-1, keepdims=True)
        rank = jnp.where(lane_k == k, rk, rank)
    rank_ref[...] = rank.astype(I32)
    carry[...] = carry[...] + jnp.sum(onehot, axis=0, keepdims=True)
    cnt_ref[...] = carry[...].astype(I32)


def _router(x1, wr, bias):
    T = x1.shape[0]
    wh = wr.astype(BF16)
    wl = (wr - wh.astype(F32)).astype(BF16)
    tri = (jnp.arange(RT_TM)[:, None] > jnp.arange(RT_TM)[None, :]).astype(BF16)
    row = pl.BlockSpec((RT_TM, LANES), lambda i: (i, 0))
    full = lambda shape: pl.BlockSpec(shape, lambda i: (0, 0))
    return pl.pallas_call(
        _router_kernel,
        out_shape=(jax.ShapeDtypeStruct((T, LANES), I32), jax.ShapeDtypeStruct((T, LANES), F32),
                   jax.ShapeDtypeStruct((T, LANES), I32), jax.ShapeDtypeStruct((1, N_EXPERTS), I32)),
        grid=(T // RT_TM,),
        in_specs=[pl.BlockSpec((RT_TM, D_MODEL), lambda i: (i, 0)),
                  full((D_MODEL, N_EXPERTS)), full((D_MODEL, N_EXPERTS)), full((1, N_EXPERTS)),
                  full((RT_TM, RT_TM))],
        out_specs=(row, row, row, full((1, N_EXPERTS))),
        scratch_shapes=[pltpu.VMEM((1, N_EXPERTS), F32)],
        compiler_params=_cparams(("arbitrary",)),
        name="router_topk",
    )(x1, wh, wl, bias[None, :], tri)


def _idx_copy(idx_hbm, idx_smem, isem, blk, slot):
    return pltpu.make_async_copy(idx_hbm.at[blk], idx_smem.at[slot], isem.at[slot])


def _row_copy(src_hbm, buf, sem, idx_smem, slot, r):
    return pltpu.make_async_copy(src_hbm.at[pl.ds(idx_smem[slot, r], 1), :],
                                 buf.at[slot, pl.ds(r, 1), :], sem.at[slot])


def _gather_step(i, n, idx_hbm, src_hbm, idx_smem, buf, isem, sem, n_rows):
    def issue_rows(slot):
        for r in range(n_rows):
            _row_copy(src_hbm, buf, sem, idx_smem, slot, r).start()

    @pl.when(i == 0)
    def _():
        _idx_copy(idx_hbm, idx_smem, isem, 0, 0).start()
        _idx_copy(idx_hbm, idx_smem, isem, 0, 0).wait()
        issue_rows(0)

        @pl.when(n > 1)
        def _():
            _idx_copy(idx_hbm, idx_smem, isem, 1, 1).start()

    nxt = (i + 1) % 2
    cur = i % 2

    @pl.when(i + 1 < n)
    def _():
        _idx_copy(idx_hbm, idx_smem, isem, i + 1, nxt).wait()
        issue_rows(nxt)

    for r in range(n_rows):
        _row_copy(src_hbm, buf, sem, idx_smem, cur, r).wait()

    @pl.when(i + 2 < n)
    def _():
        _idx_copy(idx_hbm, idx_smem, isem, i + 2, cur).start()

    return cur


def _expert_kernel(blk_e_ref, nused_ref, tok_hbm, x_hbm, wg_ref, wu_ref, wd_ref, y_ref,
                   idx_smem, buf, isem, sem):
    i = pl.program_id(0)
    n = pl.num_programs(0)
    slot = _gather_step(i, n, tok_hbm, x_hbm, idx_smem, buf, isem, sem, EXP_BLK)

    @pl.when(i < nused_ref[0])
    def _():
        xb = buf[slot].astype(BF16)
        g = jnp.dot(xb, wg_ref[0].astype(BF16), preferred_element_type=F32)
        u = jnp.dot(xb, wu_ref[0].astype(BF16), preferred_element_type=F32)
        h = (g * (1.0 / (1.0 + jnp.exp(-g))) * u).astype(BF16)
        y_ref[...] = jnp.dot(h, wd_ref[0].astype(BF16), preferred_element_type=F32)

    @pl.when(i >= nused_ref[0])
    def _():
        y_ref[...] = jnp.zeros_like(y_ref)


def _experts(x1, row_tok, blk_e, n_used, e_gate, e_up, e_down):
    nblk = row_tok.shape[0]
    wmap = lambda i, be, nu: (be[i], 0, 0)
    grid_spec = pltpu.PrefetchScalarGridSpec(
        num_scalar_prefetch=2, grid=(nblk,),
        in_specs=[pl.BlockSpec(memory_space=pl.ANY),
                  pl.BlockSpec(memory_space=pl.ANY),
                  pl.BlockSpec((1, D_MODEL, D_EXPERT), wmap),
                  pl.BlockSpec((1, D_MODEL, D_EXPERT), wmap),
                  pl.BlockSpec((1, D_EXPERT, D_MODEL), wmap)],
        out_specs=pl.BlockSpec((EXP_BLK, D_MODEL), lambda i, be, nu: (i, 0)),
        scratch_shapes=[pltpu.SMEM((2, EXP_BLK), I32),
                        pltpu.VMEM((2, EXP_BLK, D_MODEL), F32),
                        pltpu.SemaphoreType.DMA((2,)),
                        pltpu.SemaphoreType.DMA((2,))])
    return pl.pallas_call(
        _expert_kernel,
        out_shape=jax.ShapeDtypeStruct((nblk * EXP_BLK, D_MODEL), F32),
        grid_spec=grid_spec,
        compiler_params=_cparams(("arbitrary",)),
        name="moe_experts",
    )(blk_e, n_used, row_tok, x1, e_gate, e_up, e_down)


def _combine_kernel(dest_hbm, ys_hbm, x_ref, gate_ref, sg_ref, su_ref, sd_ref, g_ref, b_ref, o_ref,
                    idx_smem, buf, isem, sem):
    i = pl.program_id(0)
    n = pl.num_programs(0)
    slot = _gather_step(i, n, dest_hbm, ys_hbm, idx_smem, buf, isem, sem, CMB_TOK * TOP_K)
    x = x_ref[...]
    xb = x.astype(BF16)
    g = jnp.dot(xb, sg_ref[...], preferred_element_type=F32)
    u = jnp.dot(xb, su_ref[...], preferred_element_type=F32)
    h = (g * (1.0 / (1.0 + jnp.exp(-g))) * u).astype(BF16)
    acc = DN_ALPHA * x + jnp.dot(h, sd_ref[...], preferred_element_type=F32)
    gate = gate_ref[...]
    for k in range(TOP_K):
        acc = acc + buf[slot, k * CMB_TOK:(k + 1) * CMB_TOK, :] * gate[:, k:k + 1]
    o_ref[...] = _layer_norm(acc, g_ref[...], b_ref[...])


def _combine(ys, dest_km, x1, gates, sg, su, sd, g, b):
    T = x1.shape[0]
    full = lambda shape: pl.BlockSpec(shape, lambda i: (0, 0))
    row = pl.BlockSpec((CMB_TOK, D_MODEL), lambda i: (i, 0))
    return pl.pallas_call(
        _combine_kernel,
        out_shape=jax.ShapeDtypeStruct((T, D_MODEL), F32),
        grid=(T // CMB_TOK,),
        in_specs=[pl.BlockSpec(memory_space=pl.ANY), pl.BlockSpec(memory_space=pl.ANY),
                  row, pl.BlockSpec((CMB_TOK, LANES), lambda i: (i, 0)),
                  full((D_MODEL, D_SHARED)), full((D_MODEL, D_SHARED)), full((D_SHARED, D_MODEL)),
                  full((1, D_MODEL)), full((1, D_MODEL))],
        out_specs=row,
        scratch_shapes=[pltpu.SMEM((2, CMB_TOK * TOP_K), I32),
                        pltpu.VMEM((2, CMB_TOK * TOP_K, D_MODEL), F32),
                        pltpu.SemaphoreType.DMA((2,)),
                        pltpu.SemaphoreType.DMA((2,))],
        compiler_params=_cparams(("arbitrary",)),
        name="moe_combine_ln",
    )(dest_km, ys, x1, gates, sg, su, sd, g, b)


def _moe_ln(x1, p):
    T = x1.shape[0]
    te, gates, rank, counts = _router(x1, p['router_w'], p['router_bias'])
    top_e, rank = te[:, :TOP_K], rank[:, :TOP_K]
    counts = counts[0]
    padded = (counts + EXP_BLK - 1) // EXP_BLK * EXP_BLK
    pends = jnp.cumsum(padded)
    dest = (pends - padded)[top_e] + rank
    nblk = (T * TOP_K + N_EXPERTS * (EXP_BLK - 1)) // EXP_BLK + 1
    tok = jnp.broadcast_to(jnp.arange(T, dtype=I32)[:, None], (T, TOP_K))
    row_tok = jnp.zeros((nblk * EXP_BLK,), I32).at[dest.reshape(-1)].set(tok.reshape(-1))
    blk_e = jnp.minimum(jnp.searchsorted(pends, jnp.arange(nblk, dtype=I32) * EXP_BLK, side='right'),
                        N_EXPERTS - 1).astype(I32)
    n_used = (pends[-1] // EXP_BLK).astype(I32)[None]
    ys = _experts(x1, row_tok.reshape(nblk, EXP_BLK), blk_e, n_used,
                  p['e_gate'], p['e_up'], p['e_down'])
    dest_km = dest.reshape(T // CMB_TOK, CMB_TOK, TOP_K).transpose(0, 2, 1).reshape(
        T // CMB_TOK, CMB_TOK * TOP_K)
    return _combine(ys, dest_km, x1, gates, p['s_gate'].astype(BF16), p['s_up'].astype(BF16),
                    p['s_down'].astype(BF16), p['ln2_g'][None, :], p['ln2_b'][None, :])


def _encoder_layer(x, p, lam_init, consts, B, L):
    ca, sa, cd, sd, gmat, hy_tabs = consts
    proj = _in_proj(x, p['w_in'].astype(BF16), ca, sa, cd, sd, L)
    lam = (jnp.exp(jnp.sum(p['lam_q1'] * p['lam_k1'])) - jnp.exp(jnp.sum(p['lam_q2'] * p['lam_k2']))
           + lam_init).astype(F32)[None]
    gn = (jnp.tile(p['a_subln_g'], A_HEADS) * (1.0 - lam_init))[None, :]
    o_a = _diff_attn(proj, lam, gn, gmat, B, L)
    o_b = _hyena(proj, p, hy_tabs, B, L)
    o_c = _na_attn(proj, _na_bias(p['na_rpb'], L // GRID_W), B, L)
    o_d = _win_attn(proj, p['d_sink'], B, L)
    x1 = _out_proj(o_a, o_b, o_c, o_d, p['w_out'].astype(BF16), x,
                   p['ln1_g'][None, :], p['ln1_b'][None, :])
    return _moe_ln(x1, p)


def _trunk(x3, params):
    B, L, _ = x3.shape
    ca, sa = _rope_tables(L, A_DH)
    cd, sd = _rope_tables(L, HEAD_DIM)
    head = jnp.arange(GROUP_W) // HEAD_DIM
    gmat = ((head[:, None] == head[None, :]).astype(F32) / HEAD_DIM).astype(BF16)
    hy_tabs = _trig_tables(L, False) + _trig_tables(L, True) + _hy_positions(L)
    consts = (ca, sa, cd, sd, gmat, hy_tabs)
    x = x3.reshape(B * L, D_MODEL)
    for l in range(DEPTH):
        lp = {name: arr[l] for name, arr in params.items()}
        x = _encoder_layer(x, lp, 0.8 - 0.6 * math.exp(-0.3 * l), consts, B, L)
    return x.reshape(B, L, D_MODEL)


def kernel(x_prompt, x_sample, w_in, w_out, ln1_g, ln1_b, ln2_g, ln2_b, lam_q1, lam_k1, lam_q2,
           lam_k2, a_subln_g, hy_conv_w, hy_conv_b, hy_w1, hy_b1, hy_w2, hy_b2, hy_w3, hy_freq,
           hy_bias, na_rpb, d_sink, router_w, router_bias, e_gate, e_up, e_down, s_gate, s_up,
           s_down):
    params = {
        'w_in': w_in, 'w_out': w_out, 'ln1_g': ln1_g, 'ln1_b': ln1_b, 'ln2_g': ln2_g, 'ln2_b': ln2_b,
        'lam_q1': lam_q1, 'lam_k1': lam_k1, 'lam_q2': lam_q2, 'lam_k2': lam_k2,
        'a_subln_g': a_subln_g, 'hy_conv_w': hy_conv_w, 'hy_conv_b': hy_conv_b, 'hy_w1': hy_w1,
        'hy_b1': hy_b1, 'hy_w2': hy_w2, 'hy_b2': hy_b2, 'hy_w3': hy_w3, 'hy_freq': hy_freq,
        'hy_bias': hy_bias, 'na_rpb': na_rpb, 'd_sink': d_sink, 'router_w': router_w,
        'router_bias': router_bias, 'e_gate': e_gate, 'e_up': e_up, 'e_down': e_down,
        's_gate': s_gate, 's_up': s_up, 's_down': s_down,
    }
    return _trunk(x_prompt, params), _trunk(x_sample, params)
```

```python
import functools
import math

import jax
import jax.numpy as jnp
from jax import lax
from jax.experimental import pallas as pl
from jax.experimental.pallas import tpu as pltpu

F32 = jnp.float32
BF16 = jnp.bfloat16
I32 = jnp.int32

D_MODEL = 1024
DEPTH = 2
GRID_W = 64
HEAD_DIM = 64
GROUP_W = 256
A_HEADS = 4
A_DH = 32
B_CH = 256
HY_EMB = 33
HY_BANDS = 16
HY_FFN = 64
HY_INNER = 2
HY_TARGET = 1e-2
HY_FAST = 0.3
HY_SLOW = 1.5
C_HEADS = 4
NA_WR = 8
NA_WC = 16
D_HEADS = 4
D_KV_HEADS = 2
WIN = 128
D_IN = 2816
N_EXPERTS = 256
TOP_K = 8
D_EXPERT = 256
D_SHARED = 256
ROUTED_SCALE = 2.5
ROPE_THETA = 10000.0
LN_EPS = 1e-5
DN_ALPHA = (2 * DEPTH) ** 0.25
NEG = -1e30
LOG2E = 1.4426950408889634

LANES = 128
VMEM_LIMIT = 48 * 1024 * 1024

COL_AQ, COL_AK, COL_AV = 0, 1, 2
COL_B = 1
COL_CQ, COL_CK, COL_CV = 6, 7, 8
COL_DQ = 9
COL_DK, COL_DV = 20, 21

EXP_BLK = 128
CMB_TOK = 64


def _cparams(sem):
    return pltpu.CompilerParams(dimension_semantics=sem, vmem_limit_bytes=VMEM_LIMIT)


def _rope_lanes(y, cos, sin_signed, half):
    lane = lax.broadcasted_iota(I32, y.shape, 1)
    first = (lane % (2 * half)) < half
    rot = jnp.where(first, pltpu.roll(y, LANES - half, 1), pltpu.roll(y, half, 1))
    return y * cos + rot * sin_signed


def _in_proj_kernel(x_ref, w_ref, ca_ref, sa_ref, cd_ref, sd_ref, o_ref):
    xb = x_ref[...].astype(BF16)
    ca, sa, cd, sd = ca_ref[...], sa_ref[...], cd_ref[...], sd_ref[...]
    a_scale = (A_DH ** -0.5) * LOG2E
    hd_scale = (HEAD_DIM ** -0.5) * LOG2E
    for c in range(D_IN // 256):
        y = jnp.dot(xb, w_ref[:, c * 256:(c + 1) * 256], preferred_element_type=F32)
        for hh in range(2):
            ch = 2 * c + hh
            z = y[:, hh * LANES:(hh + 1) * LANES]
            if ch in (0, 1):
                z = _rope_lanes(z, ca, sa, A_DH // 2) * a_scale
            elif ch in (2, 3):
                z = _rope_lanes(z, ca, sa, A_DH // 2)
            elif ch in (12, 13):
                z = z * hd_scale
            elif ch in (18, 19):
                z = _rope_lanes(z, cd, sd, HEAD_DIM // 2) * hd_scale
            elif ch == 20:
                z = _rope_lanes(z, cd, sd, HEAD_DIM // 2)
            o_ref[:, ch * LANES:(ch + 1) * LANES] = z.astype(BF16)


def _in_proj(x, w_bf, ca, sa, cd, sd, L, tm=512):
    T = x.shape[0]
    nl = L // tm
    tab = pl.BlockSpec((tm, LANES), lambda i: (i % nl, 0))
    return pl.pallas_call(
        _in_proj_kernel,
        out_shape=jax.ShapeDtypeStruct((T, D_IN), BF16),
        grid=(T // tm,),
        in_specs=[pl.BlockSpec((tm, D_MODEL), lambda i: (i, 0)),
                  pl.BlockSpec((D_MODEL, D_IN), lambda i: (0, 0)),
                  tab, tab, tab, tab],
        out_specs=pl.BlockSpec((tm, D_IN), lambda i: (i, 0)),
        compiler_params=_cparams(("parallel",)),
        name="in_proj",
    )(x, w_bf, ca, sa, cd, sd)


def _rope_tables(L, dh):
    half = dh // 2
    lane = jnp.arange(LANES)
    inv_freq = ROPE_THETA ** (-(2.0 * (lane % half)).astype(F32) / dh)
    ang = jnp.arange(L, dtype=F32)[:, None] * inv_freq[None, :]
    sign = jnp.where((lane % dh) < half, -1.0, 1.0).astype(F32)
    return jnp.cos(ang), jnp.sin(ang) * sign[None, :]


def _diff_attn_kernel(lam_ref, q_ref, k_ref, v_ref, gn_ref, gm_ref, o_ref):
    lam = lam_ref[0]
    tq = q_ref.shape[0]
    q = q_ref[...]
    v = v_ref[...]
    lane = lax.broadcasted_iota(I32, (tq, LANES), 1)
    lane_o = lax.broadcasted_iota(I32, (tq, GROUP_W), 1)
    acc = jnp.zeros((tq, GROUP_W), F32)
    for h in range(A_HEADS):
        parts = []
        for c in range(2):
            comp = 2 * h + c
            j, off = comp // 4, (comp % 4) * A_DH
            qc = q[:, j * LANES:(j + 1) * LANES]
            qm = jnp.where((lane >= off) & (lane < off + A_DH), qc, jnp.zeros_like(qc))
            s = lax.dot_general(qm, k_ref[:, j * LANES:(j + 1) * LANES],
                                (((1,), (1,)), ((), ())), preferred_element_type=F32)
            m = jnp.max(s, axis=-1, keepdims=True)
            e = jnp.exp2(s - m)
            l = jnp.sum(e, axis=-1, keepdims=True)
            pv = jnp.dot(e.astype(BF16), v, preferred_element_type=F32)
            parts.append(pv * (1.0 / l))
        o_h = parts[0] - lam * parts[1]
        acc = jnp.where((lane_o >= h * HEAD_DIM) & (lane_o < (h + 1) * HEAD_DIM), o_h, acc)
    sq = acc * acc
    hi = sq.astype(BF16)
    lo = (sq - hi.astype(F32)).astype(BF16)
    ms = (jnp.dot(hi, gm_ref[...], preferred_element_type=F32)
          + jnp.dot(lo, gm_ref[...], preferred_element_type=F32))
    o_ref[...] = (acc * lax.rsqrt(ms + LN_EPS) * gn_ref[...]).astype(BF16)


def _diff_attn(proj, lam, gn, gmat, B, L, tq=256):
    nq = L // tq
    return pl.pallas_call(
        _diff_attn_kernel,
        out_shape=jax.ShapeDtypeStruct((B * L, GROUP_W), BF16),
        grid=(B, nq),
        in_specs=[pl.BlockSpec(memory_space=pltpu.SMEM),
                  pl.BlockSpec((tq, GROUP_W), lambda b, i: (b * nq + i, COL_AQ)),
                  pl.BlockSpec((L, GROUP_W), lambda b, i: (b, COL_AK)),
                  pl.BlockSpec((L, GROUP_W), lambda b, i: (b, COL_AV)),
                  pl.BlockSpec((1, GROUP_W), lambda b, i: (0, 0)),
                  pl.BlockSpec((GROUP_W, GROUP_W), lambda b, i: (0, 0))],
        out_specs=pl.BlockSpec((tq, GROUP_W), lambda b, i: (b * nq + i, 0)),
        compiler_params=_cparams(("parallel", "parallel")),
        name="diff_attn",
    )(lam, proj, proj, proj, gn, gmat)


HY_RC = 512


def _hy_prep_kernel(p_ref, w_ref, b_ref, u_ref, x0_ref):
    L = p_ref.shape[0]
    w0, w1, w2, bias = w_ref[0:1, :], w_ref[1:2, :], w_ref[2:3, :], b_ref[...]
    row = lax.broadcasted_iota(I32, (HY_RC, 3 * B_CH), 0)
    zero_row = jnp.zeros((1, 3 * B_CH), F32)
    for r in range(L // HY_RC):
        r0 = r * HY_RC
        xf = p_ref[r0:r0 + HY_RC, :].astype(F32)
        prev_row = zero_row if r == 0 else p_ref[r0 - 8:r0, :].astype(F32)[7:8, :]
        next_row = (zero_row if r0 + HY_RC == L
                    else p_ref[r0 + HY_RC:r0 + HY_RC + 8, :].astype(F32)[0:1, :])
        xp = jnp.where(row == 0, prev_row, pltpu.roll(xf, 1, 0))
        xn = jnp.where(row == HY_RC - 1, next_row, pltpu.roll(xf, HY_RC - 1, 0))
        y = bias + xp * w0 + xf * w1 + xn * w2
        x0, x1, v = y[:, :B_CH], y[:, B_CH:2 * B_CH], y[:, 2 * B_CH:]
        u_ref[r0:r0 + HY_RC, :] = (v * x1).astype(BF16)
        x0_ref[r0:r0 + HY_RC, :] = x0.astype(BF16)


def _hy_prep(proj, conv_w, conv_b, B, L):
    out = jax.ShapeDtypeStruct((B * L, B_CH), BF16)
    return pl.pallas_call(
        _hy_prep_kernel,
        out_shape=(out, out),
        grid=(B,),
        in_specs=[pl.BlockSpec((L, 3 * B_CH), lambda b: (b, COL_B)),
                  pl.BlockSpec((3, 3 * B_CH), lambda b: (0, 0)),
                  pl.BlockSpec((1, 3 * B_CH), lambda b: (0, 0))],
        out_specs=(pl.BlockSpec((L, B_CH), lambda b: (b, 0)),
                   pl.BlockSpec((L, B_CH), lambda b: (b, 0))),
        compiler_params=_cparams(("parallel",)),
        name="hy_prep",
    )(proj, conv_w, conv_b)


def _trig_table_kernel(ar_ref, ai_ref, br_ref, bi_ref, c_ref, s_ref):
    br, bi = br_ref[...], bi_ref[...]
    for th in range(c_ref.shape[1] // LANES):
        ar, ai = ar_ref[:, th:th + 1], ai_ref[:, th:th + 1]
        c_ref[:, th * LANES:(th + 1) * LANES] = (ar * br - ai * bi).astype(BF16)
        s_ref[:, th * LANES:(th + 1) * LANES] = (ar * bi + ai * br).astype(BF16)


def _trig_tables(L, transposed, tr=512):
    r = jnp.arange(L, dtype=I32)[:, None]
    hi = jnp.arange(L // LANES, dtype=I32)[None, :]
    lo = jnp.arange(LANES, dtype=I32)[None, :]
    if transposed:
        pa, pb = (2 * LANES * hi * r) % (4 * L), ((2 * lo + 1) * r) % (4 * L)
    else:
        pa, pb = ((2 * r + 1) * LANES * hi) % (4 * L), ((2 * r + 1) * lo) % (4 * L)
    ang_a = pa.astype(F32) * (math.pi / (2 * L))
    ang_b = pb.astype(F32) * (math.pi / (2 * L))
    out = jax.ShapeDtypeStruct((L, L), BF16)
    sa = pl.BlockSpec((tr, L // LANES), lambda i: (i, 0))
    sb = pl.BlockSpec((tr, LANES), lambda i: (i, 0))
    so = pl.BlockSpec((tr, L), lambda i: (i, 0))
    return pl.pallas_call(
        _trig_table_kernel, out_shape=(out, out), grid=(L // tr,),
        in_specs=[sa, sa, sb, sb], out_specs=(so, so),
        compiler_params=_cparams(("parallel",)), name="trig_tables",
    )(jnp.cos(ang_a), jnp.sin(ang_a), jnp.cos(ang_b), jnp.sin(ang_b))


def _hy_filter_kernel(z_ref, t_ref, m_ref, w1_ref, b1_ref, w2_ref, b2_ref, w3_ref, fr_ref,
                      ad_ref, k_ref):
    hp = lax.Precision.HIGHEST
    fr = fr_ref[...]
    h = jnp.sin(fr * (jnp.dot(z_ref[0], w1_ref[...], precision=hp, preferred_element_type=F32)
                      + b1_ref[...]))
    for i in range(HY_INNER):
        h = jnp.sin(fr * (jnp.dot(h, w2_ref[i], precision=hp, preferred_element_type=F32)
                          + b2_ref[i]))
    h = jnp.dot(h, w3_ref[...], precision=hp, preferred_element_type=F32)
    decay = jnp.exp(-t_ref[0] * ad_ref[...])
    half = pl.program_id(0)
    sel = jnp.where(half == 0, h[:, :B_CH], -h[:, B_CH:])
    k_ref[...] = (sel * decay * m_ref[0]).astype(BF16)


def _hy_filter(zz, tt, mm, w1p, b1, w2, b2, w3, freq, absdelta, L):
    full = lambda shape: pl.BlockSpec(shape, lambda h: (0,) * len(shape))
    return pl.pallas_call(
        _hy_filter_kernel,
        out_shape=jax.ShapeDtypeStruct((L, 2 * B_CH), BF16),
        grid=(2,),
        in_specs=[pl.BlockSpec((1, L, LANES), lambda h: (h, 0, 0)),
                  pl.BlockSpec((1, L, 1), lambda h: (h, 0, 0)),
                  pl.BlockSpec((1, L, 1), lambda h: (h, 0, 0)),
                  full((LANES, HY_FFN)), full((1, HY_FFN)),
                  full((HY_INNER, HY_FFN, HY_FFN)), full((HY_INNER, 1, HY_FFN)),
                  full((HY_FFN, 2 * B_CH)), full((1, HY_FFN)), full((1, B_CH))],
        out_specs=pl.BlockSpec((L, B_CH), lambda h: (0, h)),
        compiler_params=_cparams(("parallel",)),
        name="hy_filter",
    )(zz, tt, mm, w1p, b1, w2, b2, w3, freq, absdelta)


def _dft_fwd_kernel(c_ref, s_ref, x_ref, oc_ref, os_ref):
    x = x_ref[...]
    oc_ref[...] = jnp.dot(c_ref[...], x, preferred_element_type=F32)
    os_ref[...] = jnp.dot(s_ref[...], x, preferred_element_type=F32)


def _dft_fwd(ctab, stab, x, L, tf=512):
    n = x.shape[1]
    out = jax.ShapeDtypeStruct((L, n), F32)
    return pl.pallas_call(
        _dft_fwd_kernel, out_shape=(out, out), grid=(L // tf,),
        in_specs=[pl.BlockSpec((tf, L), lambda i: (i, 0)),
                  pl.BlockSpec((tf, L), lambda i: (i, 0)),
                  pl.BlockSpec((L, n), lambda i: (0, 0))],
        out_specs=(pl.BlockSpec((tf, n), lambda i: (i, 0)),
                   pl.BlockSpec((tf, n), lambda i: (i, 0))),
        compiler_params=_cparams(("parallel",)), name="dft_filter",
    )(ctab, stab, x)


def _dft_prod_kernel(c_ref, s_ref, u_ref, hc_ref, hs_ref, yr_ref, yi_ref):
    u = u_ref[...]
    uc = jnp.dot(c_ref[...], u, preferred_element_type=F32)
    us = jnp.dot(s_ref[...], u, preferred_element_type=F32)
    tf = uc.shape[0]
    par = lax.broadcasted_iota(I32, (tf, B_CH), 0) % 2
    sgn = jnp.where(par == 0, 1.0, -1.0).astype(F32)
    hc, hs = hc_ref[...], hs_ref[...]
    kc = hc[:, :B_CH] - sgn * hs[:, B_CH:]
    ks = hs[:, :B_CH] + sgn * hc[:, B_CH:]
    yr_ref[...] = (uc * kc - us * ks).astype(BF16)
    yi_ref[...] = (uc * ks + us * kc).astype(BF16)


def _dft_prod(ctab, stab, u, hc, hs, B, L, tf=512):
    nf = L // tf
    out = jax.ShapeDtypeStruct((B * L, B_CH), BF16)
    return pl.pallas_call(
        _dft_prod_kernel, out_shape=(out, out), grid=(nf, B),
        in_specs=[pl.BlockSpec((tf, L), lambda f, b: (f, 0)),
                  pl.BlockSpec((tf, L), lambda f, b: (f, 0)),
                  pl.BlockSpec((L, B_CH), lambda f, b: (b, 0)),
                  pl.BlockSpec((tf, 2 * B_CH), lambda f, b: (f, 0)),
                  pl.BlockSpec((tf, 2 * B_CH), lambda f, b: (f, 0))],
        out_specs=(pl.BlockSpec((tf, B_CH), lambda f, b: (b * nf + f, 0)),
                   pl.BlockSpec((tf, B_CH), lambda f, b: (b * nf + f, 0))),
        compiler_params=_cparams(("parallel", "parallel")), name="dft_fwd_prod",
    )(ctab, stab, u, hc, hs)


def _dft_inv_kernel(ct_ref, st_ref, yr_ref, yi_ref, u_ref, x0_ref, bias_ref, o_ref, *, inv_l):
    y = (jnp.dot(ct_ref[...], yr_ref[...], preferred_element_type=F32)
         + jnp.dot(st_ref[...], yi_ref[...], preferred_element_type=F32)) * inv_l
    v = y + u_ref[...].astype(F32) * bias_ref[...]
    o_ref[...] = (v * x0_ref[...].astype(F32)).astype(BF16)


def _dft_inv(ctt, stt, yr, yi, u, x0, bias, B, L, tt=512):
    nt = L // tt
    return pl.pallas_call(
        functools.partial(_dft_inv_kernel, inv_l=1.0 / L),
        out_shape=jax.ShapeDtypeStruct((B * L, B_CH), BF16), grid=(nt, B),
        in_specs=[pl.BlockSpec((tt, L), lambda t, b: (t, 0)),
                  pl.BlockSpec((tt, L), lambda t, b: (t, 0)),
                  pl.BlockSpec((L, B_CH), lambda t, b: (b, 0)),
                  pl.BlockSpec((L, B_CH), lambda t, b: (b, 0)),
                  pl.BlockSpec((tt, B_CH), lambda t, b: (b * nt + t, 0)),
                  pl.BlockSpec((tt, B_CH), lambda t, b: (b * nt + t, 0)),
                  pl.BlockSpec((1, B_CH), lambda t, b: (0, 0))],
        out_specs=pl.BlockSpec((tt, B_CH), lambda t, b: (b * nt + t, 0)),
        compiler_params=_cparams(("parallel", "parallel")), name="dft_inv",
    )(ctt, stt, yr, yi, u, x0, bias)


def _hy_positions(L):
    pos = jnp.concatenate([jnp.arange(L), L - jnp.arange(L)]) % L
    t = jnp.linspace(0.0, 1.0, L, dtype=F32)[:, None]
    w = 2.0 * math.pi * jnp.arange(L, dtype=F32)[:, None] / L
    f = jnp.linspace(1e-4, HY_BANDS - 1, HY_BANDS, dtype=F32)[None, :]
    z = jnp.concatenate([t, jnp.cos(f * w), -jnp.sin(f * w)], axis=-1)
    zz = jnp.pad(z[pos], ((0, 0), (0, LANES - HY_EMB))).reshape(2, L, LANES)
    tt = t[pos].reshape(2, L, 1)
    mm = jnp.ones((2, L, 1), F32).at[1, 0, 0].set(0.0)
    return zz, tt, mm


def _hyena(proj, p, tabs, B, L):
    ctab, stab, ctt, stt, zz, tt, mm = tabs
    u, x0 = _hy_prep(proj, p['hy_conv_w'], p['hy_conv_b'][None, :], B, L)
    max_decay = math.log(HY_TARGET) / HY_FAST
    min_decay = math.log(HY_TARGET) / HY_SLOW
    absdelta = jnp.abs(jnp.linspace(min_decay, max_decay, B_CH, dtype=F32))[None, :]
    w1p = jnp.pad(p['hy_w1'], ((0, LANES - HY_EMB), (0, 0)))
    k2 = _hy_filter(zz, tt, mm, w1p, p['hy_b1'][None, :], p['hy_w2'], p['hy_b2'][:, None, :],
                    p['hy_w3'], p['hy_freq'][None, :], absdelta, L)
    hc, hs = _dft_fwd(ctab, stab, k2, L)
    yr, yi = _dft_prod(ctab, stab, u, hc, hs, B, L)
    return _dft_inv(ctt, stt, yr, yi, u, x0, p['hy_bias'][None, :], B, L)


def _na_kernel(var_ref, q_ref, k_ref, v_ref, bias_ref, o_ref, *, rows):
    r = pl.program_id(1)
    wr = min(NA_WR, rows)
    kr0 = jnp.clip(r - wr // 2, 0, rows - wr)
    start = pl.multiple_of(kr0 * GRID_W, GRID_W)
    nk = wr * GRID_W
    q = q_ref[...]
    k = k_ref[pl.ds(start, nk), :]
    v = v_ref[pl.ds(start, nk), :]
    lane = lax.broadcasted_iota(I32, (GRID_W, LANES), 1)
    lane_o = lax.broadcasted_iota(I32, (GRID_W, GROUP_W), 1)
    qi = lax.broadcasted_iota(I32, (GRID_W, nk), 0)
    kc = lax.broadcasted_iota(I32, (GRID_W, nk), 1) % GRID_W
    c0 = jnp.clip(qi - NA_WC // 2, 0, GRID_W - NA_WC)
    ok = (kc >= c0) & (kc < c0 + NA_WC)
    acc = jnp.zeros((GRID_W, GROUP_W), F32)
    for h in range(C_HEADS):
        j, off = h // 2, (h % 2) * HEAD_DIM
        qc = q[:, j * LANES:(j + 1) * LANES]
        qm = jnp.where((lane >= off) & (lane < off + HEAD_DIM), qc, jnp.zeros_like(qc))
        s = lax.dot_general(qm, k[:, j * LANES:(j + 1) * LANES], (((1,), (1,)), ((), ())),
                            preferred_element_type=F32)
        s = jnp.where(ok, s + bias_ref[0, h], NEG)
        m = jnp.max(s, axis=-1, keepdims=True)
        e = jnp.exp2(s - m)
        l = jnp.sum(e, axis=-1, keepdims=True)
        pv = jnp.dot(e.astype(BF16), v, preferred_element_type=F32) * (1.0 / l)
        acc = jnp.where((lane_o >= h * HEAD_DIM) & (lane_o < (h + 1) * HEAD_DIM), pv, acc)
    o_ref[...] = acc.astype(BF16)


def _na_bias(rpb, rows):
    wr = min(NA_WR, rows)
    c = jnp.arange(GRID_W)
    dc = jnp.clip(c[None, :] - c[:, None], 1 - NA_WC, NA_WC - 1) + (NA_WC - 1)
    onehot = (dc[None, :, :] == jnp.arange(2 * NA_WC - 1)[:, None, None]).astype(F32)
    cols = jnp.einsum('hrd,dqk->hrqk', rpb.astype(F32) * LOG2E, onehot,
                      precision=lax.Precision.HIGHEST)
    variants = []
    for i in range(NA_WR):
        blk = cols[:, NA_WR - 1 - i:NA_WR - 1 - i + wr]
        variants.append(jnp.moveaxis(blk, 1, 2).reshape(C_HEADS, GRID_W, wr * GRID_W))
    return jnp.stack(variants)


def _na_attn(proj, bias, B, L):
    rows = L // GRID_W
    wr = min(NA_WR, rows)
    r = jnp.arange(rows)
    variant = (r - jnp.clip(r - wr // 2, 0, rows - wr)).astype(I32)
    nk = wr * GRID_W
    grid_spec = pltpu.PrefetchScalarGridSpec(
        num_scalar_prefetch=1, grid=(B, rows),
        in_specs=[pl.BlockSpec((GRID_W, GROUP_W), lambda b, r, var: (b * rows + r, COL_CQ)),
                  pl.BlockSpec((L, GROUP_W), lambda b, r, var: (b, COL_CK)),
                  pl.BlockSpec((L, GROUP_W), lambda b, r, var: (b, COL_CV)),
                  pl.BlockSpec((1, C_HEADS, GRID_W, nk), lambda b, r, var: (var[r], 0, 0, 0))],
        out_specs=pl.BlockSpec((GRID_W, GROUP_W), lambda b, r, var: (b * rows + r, 0)))
    return pl.pallas_call(
        functools.partial(_na_kernel, rows=rows),
        out_shape=jax.ShapeDtypeStruct((B * L, GROUP_W), BF16),
        grid_spec=grid_spec,
        compiler_params=_cparams(("parallel", "arbitrary")),
        name="na_attn",
    )(variant, proj, proj, proj, bias)


WQ_TILE = 256
WK_SPAN = WQ_TILE + 2 * WIN


def _win_kernel(sink_ref, q_ref, k_ref, v_ref, o_ref, *, L):
    i = pl.program_id(1)
    q0 = i * WQ_TILE
    k0 = pl.multiple_of(jnp.clip(q0 - WIN, 0, L - WK_SPAN), WIN)
    k = k_ref[pl.ds(k0, WK_SPAN), :]
    v = v_ref[pl.ds(k0, WK_SPAN), :]
    qf = q_ref[...].astype(F32)
    qpos = q0 + lax.broadcasted_iota(I32, (WQ_TILE, WK_SPAN), 0)
    kpos = k0 + lax.broadcasted_iota(I32, (WQ_TILE, WK_SPAN), 1)
    ok = jnp.abs(qpos - kpos) <= WIN
    lane = lax.broadcasted_iota(I32, (WQ_TILE, LANES), 1)
    for j in range(2):
        out = jnp.zeros((WQ_TILE, LANES), F32)
        for hh in range(2):
            h = 2 * j + hh
            g = h // (D_HEADS // D_KV_HEADS)
            qc = qf[:, j * LANES:(j + 1) * LANES]
            if hh != g:
                qc = pltpu.roll(qc, HEAD_DIM, 1)
            in_g = (lane >= g * HEAD_DIM) & (lane < (g + 1) * HEAD_DIM)
            qm = jnp.where(in_g, qc, 0.0).astype(BF16)
            s = lax.dot_general(qm, k, (((1,), (1,)), ((), ())), preferred_element_type=F32)
            s = jnp.where(ok, s, NEG)
            sk = sink_ref[h] * LOG2E
            m = jnp.maximum(jnp.max(s, axis=-1, keepdims=True), sk)
            e = jnp.exp2(s - m)
            l = jnp.sum(e, axis=-1, keepdims=True) + jnp.exp2(sk - m)
            pv = jnp.dot(e.astype(BF16), v, preferred_element_type=F32) * (1.0 / l)
            pv = jnp.where(in_g, pv, 0.0)
            if hh != g:
                pv = pltpu.roll(pv, HEAD_DIM, 1)
            out = out + pv
        o_ref[:, j * LANES:(j + 1) * LANES] = out.astype(BF16)


def _win_attn(proj, sink, B, L):
    nq = L // WQ_TILE
    return pl.pallas_call(
        functools.partial(_win_kernel, L=L),
        out_shape=jax.ShapeDtypeStruct((B * L, GROUP_W), BF16),
        grid=(B, nq),
        in_specs=[pl.BlockSpec(memory_space=pltpu.SMEM),
                  pl.BlockSpec((WQ_TILE, GROUP_W), lambda b, i: (b * nq + i, COL_DQ)),
                  pl.BlockSpec((L, LANES), lambda b, i: (b, COL_DK)),
                  pl.BlockSpec((L, LANES), lambda b, i: (b, COL_DV))],
        out_specs=pl.BlockSpec((WQ_TILE, GROUP_W), lambda b, i: (b * nq + i, 0)),
        compiler_params=_cparams(("parallel", "parallel")),
        name="win_attn",
    )(sink, proj, proj, proj)


def _layer_norm(y, g, b):
    mu = jnp.mean(y, axis=-1, keepdims=True)
    d = y - mu
    var = jnp.mean(d * d, axis=-1, keepdims=True)
    return d * lax.rsqrt(var + LN_EPS) * g + b


def _out_proj_kernel(oa_ref, ob_ref, oc_ref, od_ref, w_ref, x_ref, g_ref, b_ref, o_ref):
    acc = DN_ALPHA * x_ref[...]
    for gi, ref in enumerate((oa_ref, ob_ref, oc_ref, od_ref)):
        acc = acc + jnp.dot(ref[...], w_ref[gi * GROUP_W:(gi + 1) * GROUP_W, :],
                            preferred_element_type=F32)
    o_ref[...] = _layer_norm(acc, g_ref[...], b_ref[...])


def _out_proj(oa, ob, oc, od, w_bf, x, g, b, tm=512):
    T = x.shape[0]
    og = pl.BlockSpec((tm, GROUP_W), lambda i: (i, 0))
    row = pl.BlockSpec((tm, D_MODEL), lambda i: (i, 0))
    vec = pl.BlockSpec((1, D_MODEL), lambda i: (0, 0))
    return pl.pallas_call(
        _out_proj_kernel,
        out_shape=jax.ShapeDtypeStruct((T, D_MODEL), F32),
        grid=(T // tm,),
        in_specs=[og, og, og, og, pl.BlockSpec((D_MODEL, D_MODEL), lambda i: (0, 0)), row, vec, vec],
        out_specs=row,
        compiler_params=_cparams(("parallel",)),
        name="out_proj_ln",
    )(oa, ob, oc, od, w_bf, x, g, b)


RT_TM = 256


def _router_kernel(x_ref, wh_ref, wl_ref, bias_ref, tri_ref, te_ref, gate_ref, rank_ref, cnt_ref,
                   carry):
    @pl.when(pl.program_id(0) == 0)
    def _():
        carry[...] = jnp.zeros_like(carry)

    x = x_ref[...]
    xh = x.astype(BF16)
    xl = (x - xh.astype(F32)).astype(BF16)
    logits = (jnp.dot(xh, wh_ref[...], preferred_element_type=F32)
              + jnp.dot(xh, wl_ref[...], preferred_element_type=F32)
              + jnp.dot(xl, wh_ref[...], preferred_element_type=F32))
    scores = 1.0 / (1.0 + jnp.exp(-logits))
    sel = scores + bias_ref[...]
    lane_e = lax.broadcasted_iota(I32, (RT_TM, N_EXPERTS), 1).astype(F32)
    lane_k = lax.broadcasted_iota(I32, (RT_TM, LANES), 1)
    te = jnp.zeros((RT_TM, LANES), F32)
    ts = jnp.zeros((RT_TM, LANES), F32)
    onehot = jnp.zeros((RT_TM, N_EXPERTS), F32)
    for k in range(TOP_K):
        m = jnp.max(sel, axis=-1, keepdims=True)
        idx = jnp.min(jnp.where(sel == m, lane_e, float(N_EXPERTS)), axis=-1, keepdims=True)
        hit = lane_e == idx
        sc = jnp.sum(jnp.where(hit, scores, 0.0), axis=-1, keepdims=True)
        te = jnp.where(lane_k == k, idx, te)
        ts = jnp.where(lane_k == k, sc, ts)
        onehot = jnp.where(hit, 1.0, onehot)
        sel = jnp.where(hit, -jnp.inf, sel)
    gate_ref[...] = ts / jnp.sum(ts, axis=-1, keepdims=True) * ROUTED_SCALE
    te_ref[...] = te.astype(I32)
    before = jnp.dot(tri_ref[...], onehot.astype(BF16), preferred_element_type=F32) + carry[...]
    rank = jnp.zeros((RT_TM, LANES), F32)
    for k in range(TOP_K):
        rk = jnp.sum(jnp.where(lane_e == te[:, k:k + 1], before, 0.0), axis=-1, keepdims=True)
        rank = jnp.where(lane_k == k, rk, rank)
    rank_ref[...] = rank.astype(I32)
    carry[...] = carry[...] + jnp.sum(onehot, axis=0, keepdims=True)
    cnt_ref[...] = carry[...].astype(I32)


def _router(x1, wr, bias):
    T = x1.shape[0]
    wh = wr.astype(BF16)
    wl = (wr - wh.astype(F32)).astype(BF16)
    tri = (jnp.arange(RT_TM)[:, None] > jnp.arange(RT_TM)[None, :]).astype(BF16)
    row = pl.BlockSpec((RT_TM, LANES), lambda i: (i, 0))
    full = lambda shape: pl.BlockSpec(shape, lambda i: (0, 0))
    return pl.pallas_call(
        _router_kernel,
        out_shape=(jax.ShapeDtypeStruct((T, LANES), I32), jax.ShapeDtypeStruct((T, LANES), F32),
                   jax.ShapeDtypeStruct((T, LANES), I32), jax.ShapeDtypeStruct((1, N_EXPERTS), I32)),
        grid=(T // RT_TM,),
        in_specs=[pl.BlockSpec((RT_TM, D_MODEL), lambda i: (i, 0)),
                  full((D_MODEL, N_EXPERTS)), full((D_MODEL, N_EXPERTS)), full((1, N_EXPERTS)),
                  full((RT_TM, RT_TM))],
        out_specs=(row, row, row, full((1, N_EXPERTS))),
        scratch_shapes=[pltpu.VMEM((1, N_EXPERTS), F32)],
        compiler_params=_cparams(("arbitrary",)),
        name="router_topk",
    )(x1, wh, wl, bias[None, :], tri)


def _idx_copy(idx_hbm, idx_smem, isem, blk, slot):
    return pltpu.make_async_copy(idx_hbm.at[blk], idx_smem.at[slot], isem.at[slot])


ROW_SUB = D_MODEL // LANES


def _row_copy(src_hbm, buf, sem, idx_smem, slot, r):
    src = pl.multiple_of(idx_smem[slot, r] * ROW_SUB, ROW_SUB)
    return pltpu.make_async_copy(src_hbm.at[pl.ds(src, ROW_SUB), :],
                                 buf.at[slot, pl.ds(r * ROW_SUB, ROW_SUB), :], sem.at[slot])


def _tile_rows_chunk(buf_slot, row0, n_rows, c):
    return buf_slot[pl.ds(row0 * ROW_SUB + c, n_rows, stride=ROW_SUB), :]


def _gather_step(i, n, idx_hbm, src_hbm, idx_smem, buf, isem, sem, n_rows):
    def issue_rows(slot):
        for r in range(n_rows):
            _row_copy(src_hbm, buf, sem, idx_smem, slot, r).start()

    @pl.when(i == 0)
    def _():
        _idx_copy(idx_hbm, idx_smem, isem, 0, 0).start()
        _idx_copy(idx_hbm, idx_smem, isem, 0, 0).wait()
        issue_rows(0)

        @pl.when(n > 1)
        def _():
            _idx_copy(idx_hbm, idx_smem, isem, 1, 1).start()

    nxt = (i + 1) % 2
    cur = i % 2

    @pl.when(i + 1 < n)
    def _():
        _idx_copy(idx_hbm, idx_smem, isem, i + 1, nxt).wait()
        issue_rows(nxt)

    for r in range(n_rows):
        _row_copy(src_hbm, buf, sem, idx_smem, cur, r).wait()

    @pl.when(i + 2 < n)
    def _():
        _idx_copy(idx_hbm, idx_smem, isem, i + 2, cur).start()

    return cur


def _expert_kernel(blk_e_ref, nused_ref, tok_hbm, x_hbm, wg_ref, wu_ref, wd_ref, y_ref,
                   idx_smem, buf, isem, sem):
    i = pl.program_id(0)
    n = pl.num_programs(0)
    slot = _gather_step(i, n, tok_hbm, x_hbm, idx_smem, buf, isem, sem, EXP_BLK)

    @pl.when(i < nused_ref[0])
    def _():
        xb = jnp.concatenate([_tile_rows_chunk(buf.at[slot], 0, EXP_BLK, c) for c in range(ROW_SUB)],
                             axis=1).astype(BF16)
        g = jnp.dot(xb, wg_ref[0, 0].astype(BF16), preferred_element_type=F32)
        u = jnp.dot(xb, wu_ref[0, 0].astype(BF16), preferred_element_type=F32)
        h = (g * (1.0 / (1.0 + jnp.exp(-g))) * u).astype(BF16)
        y = jnp.dot(h, wd_ref[0, 0].astype(BF16), preferred_element_type=F32)
        for c in range(ROW_SUB):
            y_ref[pl.ds(c, EXP_BLK, stride=ROW_SUB), :] = y[:, c * LANES:(c + 1) * LANES]

    @pl.when(i >= nused_ref[0])
    def _():
        y_ref[...] = jnp.zeros_like(y_ref)


def _experts(x1_tiles, row_tok, blk_e, n_used, e_gate, e_up, e_down, layer):
    nblk = row_tok.shape[0]
    wmap = lambda i, be, nu: (layer, be[i], 0, 0)
    grid_spec = pltpu.PrefetchScalarGridSpec(
        num_scalar_prefetch=2, grid=(nblk,),
        in_specs=[pl.BlockSpec(memory_space=pl.ANY),
                  pl.BlockSpec(memory_space=pl.ANY),
                  pl.BlockSpec((1, 1, D_MODEL, D_EXPERT), wmap),
                  pl.BlockSpec((1, 1, D_MODEL, D_EXPERT), wmap),
                  pl.BlockSpec((1, 1, D_EXPERT, D_MODEL), wmap)],
        out_specs=pl.BlockSpec((EXP_BLK * ROW_SUB, LANES), lambda i, be, nu: (i, 0)),
        scratch_shapes=[pltpu.SMEM((2, EXP_BLK), I32),
                        pltpu.VMEM((2, EXP_BLK * ROW_SUB, LANES), F32),
                        pltpu.SemaphoreType.DMA((2,)),
                        pltpu.SemaphoreType.DMA((2,))])
    return pl.pallas_call(
        _expert_kernel,
        out_shape=jax.ShapeDtypeStruct((nblk * EXP_BLK * ROW_SUB, LANES), F32),
        grid_spec=grid_spec,
        compiler_params=_cparams(("arbitrary",)),
        name="moe_experts",
    )(blk_e, n_used, row_tok, x1_tiles, e_gate, e_up, e_down)


def _combine_kernel(dest_hbm, ys_hbm, x_ref, gate_ref, sg_ref, su_ref, sd_ref, g_ref, b_ref, o_ref,
                    idx_smem, buf, isem, sem):
    i = pl.program_id(0)
    n = pl.num_programs(0)
    slot = _gather_step(i, n, dest_hbm, ys_hbm, idx_smem, buf, isem, sem, CMB_TOK * TOP_K)
    x = x_ref[...]
    xb = x.astype(BF16)
    g = jnp.dot(xb, sg_ref[...], preferred_element_type=F32)
    u = jnp.dot(xb, su_ref[...], preferred_element_type=F32)
    h = (g * (1.0 / (1.0 + jnp.exp(-g))) * u).astype(BF16)
    acc = DN_ALPHA * x + jnp.dot(h, sd_ref[...], preferred_element_type=F32)
    gate = gate_ref[...]
    chunks = []
    for c in range(ROW_SUB):
        part = jnp.zeros((CMB_TOK, LANES), F32)
        for k in range(TOP_K):
            part = part + _tile_rows_chunk(buf.at[slot], k * CMB_TOK, CMB_TOK, c) * gate[:, k:k + 1]
        chunks.append(part)
    acc = acc + jnp.concatenate(chunks, axis=1)
    o_ref[...] = _layer_norm(acc, g_ref[...], b_ref[...])


def _combine(ys, dest_km, x1, gates, sg, su, sd, g, b):
    T = x1.shape[0]
    full = lambda shape: pl.BlockSpec(shape, lambda i: (0, 0))
    row = pl.BlockSpec((CMB_TOK, D_MODEL), lambda i: (i, 0))
    return pl.pallas_call(
        _combine_kernel,
        out_shape=jax.ShapeDtypeStruct((T, D_MODEL), F32),
        grid=(T // CMB_TOK,),
        in_specs=[pl.BlockSpec(memory_space=pl.ANY), pl.BlockSpec(memory_space=pl.ANY),
                  row, pl.BlockSpec((CMB_TOK, LANES), lambda i: (i, 0)),
                  full((D_MODEL, D_SHARED)), full((D_MODEL, D_SHARED)), full((D_SHARED, D_MODEL)),
                  full((1, D_MODEL)), full((1, D_MODEL))],
        out_specs=row,
        scratch_shapes=[pltpu.SMEM((2, CMB_TOK * TOP_K), I32),
                        pltpu.VMEM((2, CMB_TOK * TOP_K * ROW_SUB, LANES), F32),
                        pltpu.SemaphoreType.DMA((2,)),
                        pltpu.SemaphoreType.DMA((2,))],
        compiler_params=_cparams(("arbitrary",)),
        name="moe_combine_ln",
    )(dest_km, ys, x1, gates, sg, su, sd, g, b)


DEST_TM = 1024


def _dest_kernel(te_ref, rank_ref, ps_ref, d_ref):
    te = te_ref[...]
    lane_e = lax.broadcasted_iota(I32, (DEST_TM, N_EXPERTS), 1)
    lane_k = lax.broadcasted_iota(I32, (DEST_TM, LANES), 1)
    dest = jnp.zeros((DEST_TM, LANES), F32)
    for k in range(TOP_K):
        v = jnp.sum(jnp.where(lane_e == te[:, k:k + 1], ps_ref[...], 0.0), axis=-1, keepdims=True)
        dest = jnp.where(lane_k == k, v, dest)
    d_ref[...] = dest.astype(I32) + rank_ref[...]


def _dest_rows(te, rank, pstart):
    T = te.shape[0]
    row = pl.BlockSpec((DEST_TM, LANES), lambda i: (i, 0))
    return pl.pallas_call(
        _dest_kernel, out_shape=jax.ShapeDtypeStruct((T, LANES), I32), grid=(T // DEST_TM,),
        in_specs=[row, row, pl.BlockSpec((1, N_EXPERTS), lambda i: (0, 0))], out_specs=row,
        compiler_params=_cparams(("parallel",)), name="moe_dest",
    )(te, rank, pstart)


def _moe_ln(x1, p, stacked, layer):
    T = x1.shape[0]
    te, gates, rank, counts = _router(x1, p['router_w'], p['router_bias'])
    counts = counts[0]
    padded = (counts + EXP_BLK - 1) // EXP_BLK * EXP_BLK
    pends = jnp.cumsum(padded)
    dest = _dest_rows(te, rank, (pends - padded).astype(F32)[None, :])[:, :TOP_K]
    nblk = (T * TOP_K + N_EXPERTS * (EXP_BLK - 1)) // EXP_BLK + 1
    tok = jnp.broadcast_to(jnp.arange(T, dtype=I32)[:, None], (T, TOP_K))
    row_tok = jnp.zeros((nblk * EXP_BLK,), I32).at[dest.reshape(-1)].set(tok.reshape(-1))
    blk_start = jnp.arange(nblk, dtype=I32) * EXP_BLK
    blk_e = jnp.minimum(jnp.sum((pends[None, :] <= blk_start[:, None]).astype(I32), axis=1),
                        N_EXPERTS - 1)
    n_used = (pends[-1] // EXP_BLK).astype(I32)[None]
    x1_tiles = x1.reshape(T * ROW_SUB, LANES)
    ys = _experts(x1_tiles, row_tok.reshape(nblk, EXP_BLK), blk_e, n_used,
                  stacked['e_gate'], stacked['e_up'], stacked['e_down'], layer)
    dest_km = dest.reshape(T // CMB_TOK, CMB_TOK, TOP_K).transpose(0, 2, 1).reshape(
        T // CMB_TOK, CMB_TOK * TOP_K)
    return _combine(ys, dest_km, x1, gates, p['s_gate'].astype(BF16), p['s_up'].astype(BF16),
                    p['s_down'].astype(BF16), p['ln2_g'][None, :], p['ln2_b'][None, :])


STACKED = ('e_gate', 'e_up', 'e_down')


def _encoder_layer(x, p, stacked, layer, lam_init, consts, B, L):
    ca, sa, cd, sd, gmat, hy_tabs = consts
    proj = _in_proj(x, p['w_in'].astype(BF16), ca, sa, cd, sd, L)
    lam = (jnp.exp(jnp.sum(p['lam_q1'] * p['lam_k1'])) - jnp.exp(jnp.sum(p['lam_q2'] * p['lam_k2']))
           + lam_init).astype(F32)[None]
    gn = (jnp.tile(p['a_subln_g'], A_HEADS) * (1.0 - lam_init))[None, :]
    o_a = _diff_attn(proj, lam, gn, gmat, B, L)
    o_b = _hyena(proj, p, hy_tabs, B, L)
    o_c = _na_attn(proj, _na_bias(p['na_rpb'], L // GRID_W), B, L)
    o_d = _win_attn(proj, p['d_sink'], B, L)
    x1 = _out_proj(o_a, o_b, o_c, o_d, p['w_out'].astype(BF16), x,
                   p['ln1_g'][None, :], p['ln1_b'][None, :])
    return _moe_ln(x1, p, stacked, layer)


def _trunk(x3, params):
    B, L, _ = x3.shape
    ca, sa = _rope_tables(L, A_DH)
    cd, sd = _rope_tables(L, HEAD_DIM)
    head = jnp.arange(GROUP_W) // HEAD_DIM
    gmat = ((head[:, None] == head[None, :]).astype(F32) / HEAD_DIM).astype(BF16)
    hy_tabs = _trig_tables(L, False) + _trig_tables(L, True) + _hy_positions(L)
    consts = (ca, sa, cd, sd, gmat, hy_tabs)
    x = x3.reshape(B * L, D_MODEL)
    stacked = {name: params[name] for name in STACKED}
    for l in range(DEPTH):
        lp = {name: arr[l] for name, arr in params.items() if name not in STACKED}
        x = _encoder_layer(x, lp, stacked, l, 0.8 - 0.6 * math.exp(-0.3 * l), consts, B, L)
    return x.reshape(B, L, D_MODEL)


def kernel(x_prompt, x_sample, w_in, w_out, ln1_g, ln1_b, ln2_g, ln2_b, lam_q1, lam_k1, lam_q2,
           lam_k2, a_subln_g, hy_conv_w, hy_conv_b, hy_w1, hy_b1, hy_w2, hy_b2, hy_w3, hy_freq,
           hy_bias, na_rpb, d_sink, router_w, router_bias, e_gate, e_up, e_down, s_gate, s_up,
           s_down):
    params = {
        'w_in': w_in, 'w_out': w_out, 'ln1_g': ln1_g, 'ln1_b': ln1_b, 'ln2_g': ln2_g, 'ln2_b': ln2_b,
        'lam_q1': lam_q1, 'lam_k1': lam_k1, 'lam_q2': lam_q2, 'lam_k2': lam_k2,
        'a_subln_g': a_subln_g, 'hy_conv_w': hy_conv_w, 'hy_conv_b': hy_conv_b, 'hy_w1': hy_w1,
        'hy_b1': hy_b1, 'hy_w2': hy_w2, 'hy_b2': hy_b2, 'hy_w3': hy_w3, 'hy_freq': hy_freq,
        'hy_bias': hy_bias, 'na_rpb': na_rpb, 'd_sink': d_sink, 'router_w': router_w,
        'router_bias': router_bias, 'e_gate': e_gate, 'e_up': e_up, 'e_down': e_down,
        's_gate': s_gate, 's_up': s_up, 's_down': s_down,
    }
    return _trunk(x_prompt, params), _trunk(x_sample, params)
```

```python
import functools
import math

import jax
import jax.numpy as jnp
from jax import lax
from jax.experimental import pallas as pl
from jax.experimental.pallas import tpu as pltpu

F32 = jnp.float32
BF16 = jnp.bfloat16
I32 = jnp.int32

D_MODEL = 1024
DEPTH = 2
GRID_W = 64
HEAD_DIM = 64
GROUP_W = 256
A_HEADS = 4
A_DH = 32
B_CH = 256
HY_EMB = 33
HY_BANDS = 16
HY_FFN = 64
HY_INNER = 2
HY_TARGET = 1e-2
HY_FAST = 0.3
HY_SLOW = 1.5
C_HEADS = 4
NA_WR = 8
NA_WC = 16
D_HEADS = 4
D_KV_HEADS = 2
WIN = 128
D_IN = 2816
N_EXPERTS = 256
TOP_K = 8
D_EXPERT = 256
D_SHARED = 256
ROUTED_SCALE = 2.5
ROPE_THETA = 10000.0
LN_EPS = 1e-5
DN_ALPHA = (2 * DEPTH) ** 0.25
NEG = -1e30
LOG2E = 1.4426950408889634

LANES = 128
VMEM_LIMIT = 48 * 1024 * 1024

COL_AQ, COL_AK, COL_AV = 0, 1, 2
COL_B = 1
COL_CQ, COL_CK, COL_CV = 6, 7, 8
COL_DQ = 9
COL_DK, COL_DV = 20, 21

EXP_BLK = 256
CMB_TOK = 64


def _cparams(sem):
    return pltpu.CompilerParams(dimension_semantics=sem, vmem_limit_bytes=VMEM_LIMIT)


def _rope_lanes(y, cos, sin_signed, half):
    lane = lax.broadcasted_iota(I32, y.shape, 1)
    first = (lane % (2 * half)) < half
    rot = jnp.where(first, pltpu.roll(y, LANES - half, 1), pltpu.roll(y, half, 1))
    return y * cos + rot * sin_signed


def _in_proj_kernel(x_ref, w_ref, ca_ref, sa_ref, cd_ref, sd_ref, o_ref):
    xb = x_ref[...].astype(BF16)
    ca, sa, cd, sd = ca_ref[...], sa_ref[...], cd_ref[...], sd_ref[...]
    a_scale = (A_DH ** -0.5) * LOG2E
    hd_scale = (HEAD_DIM ** -0.5) * LOG2E
    for c in range(D_IN // 256):
        y = jnp.dot(xb, w_ref[:, c * 256:(c + 1) * 256], preferred_element_type=F32)
        for hh in range(2):
            ch = 2 * c + hh
            z = y[:, hh * LANES:(hh + 1) * LANES]
            if ch in (0, 1):
                z = _rope_lanes(z, ca, sa, A_DH // 2) * a_scale
            elif ch in (2, 3):
                z = _rope_lanes(z, ca, sa, A_DH // 2)
            elif ch in (12, 13):
                z = z * hd_scale
            elif ch in (18, 19):
                z = _rope_lanes(z, cd, sd, HEAD_DIM // 2) * hd_scale
            elif ch == 20:
                z = _rope_lanes(z, cd, sd, HEAD_DIM // 2)
            o_ref[:, ch * LANES:(ch + 1) * LANES] = z.astype(BF16)


def _in_proj(x, w_bf, ca, sa, cd, sd, L, tm=512):
    T = x.shape[0]
    nl = L // tm
    tab = pl.BlockSpec((tm, LANES), lambda i: (i % nl, 0))
    return pl.pallas_call(
        _in_proj_kernel,
        out_shape=jax.ShapeDtypeStruct((T, D_IN), BF16),
        grid=(T // tm,),
        in_specs=[pl.BlockSpec((tm, D_MODEL), lambda i: (i, 0)),
                  pl.BlockSpec((D_MODEL, D_IN), lambda i: (0, 0)),
                  tab, tab, tab, tab],
        out_specs=pl.BlockSpec((tm, D_IN), lambda i: (i, 0)),
        compiler_params=_cparams(("parallel",)),
        name="in_proj",
    )(x, w_bf, ca, sa, cd, sd)


def _rope_tables(L, dh):
    half = dh // 2
    lane = jnp.arange(LANES)
    inv_freq = ROPE_THETA ** (-(2.0 * (lane % half)).astype(F32) / dh)
    ang = jnp.arange(L, dtype=F32)[:, None] * inv_freq[None, :]
    sign = jnp.where((lane % dh) < half, -1.0, 1.0).astype(F32)
    return jnp.cos(ang), jnp.sin(ang) * sign[None, :]


def _diff_attn_kernel(lam_ref, q_ref, k_ref, v_ref, gn_ref, gm_ref, o_ref):
    lam = lam_ref[0]
    tq = q_ref.shape[0]
    q = q_ref[...]
    v = v_ref[...]
    lane = lax.broadcasted_iota(I32, (tq, LANES), 1)
    lane_o = lax.broadcasted_iota(I32, (tq, GROUP_W), 1)
    acc = jnp.zeros((tq, GROUP_W), F32)
    for h in range(A_HEADS):
        parts = []
        for c in range(2):
            comp = 2 * h + c
            j, off = comp // 4, (comp % 4) * A_DH
            qc = q[:, j * LANES:(j + 1) * LANES]
            qm = jnp.where((lane >= off) & (lane < off + A_DH), qc, jnp.zeros_like(qc))
            s = lax.dot_general(qm, k_ref[:, j * LANES:(j + 1) * LANES],
                                (((1,), (1,)), ((), ())), preferred_element_type=F32)
            m = jnp.max(s, axis=-1, keepdims=True)
            e = jnp.exp2(s - m)
            l = jnp.sum(e, axis=-1, keepdims=True)
            pv = jnp.dot(e.astype(BF16), v, preferred_element_type=F32)
            parts.append(pv * (1.0 / l))
        o_h = parts[0] - lam * parts[1]
        acc = jnp.where((lane_o >= h * HEAD_DIM) & (lane_o < (h + 1) * HEAD_DIM), o_h, acc)
    sq = acc * acc
    hi = sq.astype(BF16)
    lo = (sq - hi.astype(F32)).astype(BF16)
    ms = (jnp.dot(hi, gm_ref[...], preferred_element_type=F32)
          + jnp.dot(lo, gm_ref[...], preferred_element_type=F32))
    o_ref[...] = (acc * lax.rsqrt(ms + LN_EPS) * gn_ref[...]).astype(BF16)


def _diff_attn(proj, lam, gn, gmat, B, L, tq=256):
    nq = L // tq
    return pl.pallas_call(
        _diff_attn_kernel,
        out_shape=jax.ShapeDtypeStruct((B * L, GROUP_W), BF16),
        grid=(B, nq),
        in_specs=[pl.BlockSpec(memory_space=pltpu.SMEM),
                  pl.BlockSpec((tq, GROUP_W), lambda b, i: (b * nq + i, COL_AQ)),
                  pl.BlockSpec((L, GROUP_W), lambda b, i: (b, COL_AK)),
                  pl.BlockSpec((L, GROUP_W), lambda b, i: (b, COL_AV)),
                  pl.BlockSpec((1, GROUP_W), lambda b, i: (0, 0)),
                  pl.BlockSpec((GROUP_W, GROUP_W), lambda b, i: (0, 0))],
        out_specs=pl.BlockSpec((tq, GROUP_W), lambda b, i: (b * nq + i, 0)),
        compiler_params=_cparams(("parallel", "parallel")),
        name="diff_attn",
    )(lam, proj, proj, proj, gn, gmat)


HY_RC = 512


def _hy_prep_kernel(p_ref, w_ref, b_ref, u_ref, x0_ref):
    L = p_ref.shape[0]
    w0, w1, w2, bias = w_ref[0:1, :], w_ref[1:2, :], w_ref[2:3, :], b_ref[...]
    row = lax.broadcasted_iota(I32, (HY_RC, 3 * B_CH), 0)
    zero_row = jnp.zeros((1, 3 * B_CH), F32)
    for r in range(L // HY_RC):
        r0 = r * HY_RC
        xf = p_ref[r0:r0 + HY_RC, :].astype(F32)
        prev_row = zero_row if r == 0 else p_ref[r0 - 8:r0, :].astype(F32)[7:8, :]
        next_row = (zero_row if r0 + HY_RC == L
                    else p_ref[r0 + HY_RC:r0 + HY_RC + 8, :].astype(F32)[0:1, :])
        xp = jnp.where(row == 0, prev_row, pltpu.roll(xf, 1, 0))
        xn = jnp.where(row == HY_RC - 1, next_row, pltpu.roll(xf, HY_RC - 1, 0))
        y = bias + xp * w0 + xf * w1 + xn * w2
        x0, x1, v = y[:, :B_CH], y[:, B_CH:2 * B_CH], y[:, 2 * B_CH:]
        u_ref[r0:r0 + HY_RC, :] = (v * x1).astype(BF16)
        x0_ref[r0:r0 + HY_RC, :] = x0.astype(BF16)


def _hy_prep(proj, conv_w, conv_b, B, L):
    out = jax.ShapeDtypeStruct((B * L, B_CH), BF16)
    return pl.pallas_call(
        _hy_prep_kernel,
        out_shape=(out, out),
        grid=(B,),
        in_specs=[pl.BlockSpec((L, 3 * B_CH), lambda b: (b, COL_B)),
                  pl.BlockSpec((3, 3 * B_CH), lambda b: (0, 0)),
                  pl.BlockSpec((1, 3 * B_CH), lambda b: (0, 0))],
        out_specs=(pl.BlockSpec((L, B_CH), lambda b: (b, 0)),
                   pl.BlockSpec((L, B_CH), lambda b: (b, 0))),
        compiler_params=_cparams(("parallel",)),
        name="hy_prep",
    )(proj, conv_w, conv_b)


def _trig_table_kernel(ar_ref, ai_ref, br_ref, bi_ref, c_ref, s_ref):
    br, bi = br_ref[...], bi_ref[...]
    for th in range(c_ref.shape[1] // LANES):
        ar, ai = ar_ref[:, th:th + 1], ai_ref[:, th:th + 1]
        c_ref[:, th * LANES:(th + 1) * LANES] = (ar * br - ai * bi).astype(BF16)
        s_ref[:, th * LANES:(th + 1) * LANES] = (ar * bi + ai * br).astype(BF16)


def _trig_tables(L, transposed, tr=512):
    r = jnp.arange(L, dtype=I32)[:, None]
    hi = jnp.arange(L // LANES, dtype=I32)[None, :]
    lo = jnp.arange(LANES, dtype=I32)[None, :]
    if transposed:
        pa, pb = (2 * LANES * hi * r) % (4 * L), ((2 * lo + 1) * r) % (4 * L)
    else:
        pa, pb = ((2 * r + 1) * LANES * hi) % (4 * L), ((2 * r + 1) * lo) % (4 * L)
    ang_a = pa.astype(F32) * (math.pi / (2 * L))
    ang_b = pb.astype(F32) * (math.pi / (2 * L))
    out = jax.ShapeDtypeStruct((L, L), BF16)
    sa = pl.BlockSpec((tr, L // LANES), lambda i: (i, 0))
    sb = pl.BlockSpec((tr, LANES), lambda i: (i, 0))
    so = pl.BlockSpec((tr, L), lambda i: (i, 0))
    return pl.pallas_call(
        _trig_table_kernel, out_shape=(out, out), grid=(L // tr,),
        in_specs=[sa, sa, sb, sb], out_specs=(so, so),
        compiler_params=_cparams(("parallel",)), name="trig_tables",
    )(jnp.cos(ang_a), jnp.sin(ang_a), jnp.cos(ang_b), jnp.sin(ang_b))


def _hy_filter_kernel(z_ref, t_ref, m_ref, w1_ref, b1_ref, w2_ref, b2_ref, w3_ref, fr_ref,
                      ad_ref, k_ref):
    hp = lax.Precision.HIGHEST
    fr = fr_ref[...]
    h = jnp.sin(fr * (jnp.dot(z_ref[0], w1_ref[...], precision=hp, preferred_element_type=F32)
                      + b1_ref[...]))
    for i in range(HY_INNER):
        h = jnp.sin(fr * (jnp.dot(h, w2_ref[i], precision=hp, preferred_element_type=F32)
                          + b2_ref[i]))
    h = jnp.dot(h, w3_ref[...], precision=hp, preferred_element_type=F32)
    decay = jnp.exp(-t_ref[0] * ad_ref[...])
    half = pl.program_id(0)
    sel = jnp.where(half == 0, h[:, :B_CH], -h[:, B_CH:])
    k_ref[...] = (sel * decay * m_ref[0]).astype(BF16)


def _hy_filter(zz, tt, mm, w1p, b1, w2, b2, w3, freq, absdelta, L):
    full = lambda shape: pl.BlockSpec(shape, lambda h: (0,) * len(shape))
    return pl.pallas_call(
        _hy_filter_kernel,
        out_shape=jax.ShapeDtypeStruct((L, 2 * B_CH), BF16),
        grid=(2,),
        in_specs=[pl.BlockSpec((1, L, LANES), lambda h: (h, 0, 0)),
                  pl.BlockSpec((1, L, 1), lambda h: (h, 0, 0)),
                  pl.BlockSpec((1, L, 1), lambda h: (h, 0, 0)),
                  full((LANES, HY_FFN)), full((1, HY_FFN)),
                  full((HY_INNER, HY_FFN, HY_FFN)), full((HY_INNER, 1, HY_FFN)),
                  full((HY_FFN, 2 * B_CH)), full((1, HY_FFN)), full((1, B_CH))],
        out_specs=pl.BlockSpec((L, B_CH), lambda h: (0, h)),
        compiler_params=_cparams(("parallel",)),
        name="hy_filter",
    )(zz, tt, mm, w1p, b1, w2, b2, w3, freq, absdelta)


def _dft_fwd_kernel(c_ref, s_ref, x_ref, oc_ref, os_ref):
    x = x_ref[...]
    oc_ref[...] = jnp.dot(c_ref[...], x, preferred_element_type=F32)
    os_ref[...] = jnp.dot(s_ref[...], x, preferred_element_type=F32)


def _dft_fwd(ctab, stab, x, L, tf=512):
    n = x.shape[1]
    out = jax.ShapeDtypeStruct((L, n), F32)
    return pl.pallas_call(
        _dft_fwd_kernel, out_shape=(out, out), grid=(L // tf,),
        in_specs=[pl.BlockSpec((tf, L), lambda i: (i, 0)),
                  pl.BlockSpec((tf, L), lambda i: (i, 0)),
                  pl.BlockSpec((L, n), lambda i: (0, 0))],
        out_specs=(pl.BlockSpec((tf, n), lambda i: (i, 0)),
                   pl.BlockSpec((tf, n), lambda i: (i, 0))),
        compiler_params=_cparams(("parallel",)), name="dft_filter",
    )(ctab, stab, x)


def _dft_prod_kernel(c_ref, s_ref, u_ref, hc_ref, hs_ref, yr_ref, yi_ref):
    u = u_ref[...]
    uc = jnp.dot(c_ref[...], u, preferred_element_type=F32)
    us = jnp.dot(s_ref[...], u, preferred_element_type=F32)
    tf = uc.shape[0]
    par = lax.broadcasted_iota(I32, (tf, B_CH), 0) % 2
    sgn = jnp.where(par == 0, 1.0, -1.0).astype(F32)
    hc, hs = hc_ref[...], hs_ref[...]
    kc = hc[:, :B_CH] - sgn * hs[:, B_CH:]
    ks = hs[:, :B_CH] + sgn * hc[:, B_CH:]
    yr_ref[...] = (uc * kc - us * ks).astype(BF16)
    yi_ref[...] = (uc * ks + us * kc).astype(BF16)


def _dft_prod(ctab, stab, u, hc, hs, B, L, tf=512):
    nf = L // tf
    out = jax.ShapeDtypeStruct((B * L, B_CH), BF16)
    return pl.pallas_call(
        _dft_prod_kernel, out_shape=(out, out), grid=(nf, B),
        in_specs=[pl.BlockSpec((tf, L), lambda f, b: (f, 0)),
                  pl.BlockSpec((tf, L), lambda f, b: (f, 0)),
                  pl.BlockSpec((L, B_CH), lambda f, b: (b, 0)),
                  pl.BlockSpec((tf, 2 * B_CH), lambda f, b: (f, 0)),
                  pl.BlockSpec((tf, 2 * B_CH), lambda f, b: (f, 0))],
        out_specs=(pl.BlockSpec((tf, B_CH), lambda f, b: (b * nf + f, 0)),
                   pl.BlockSpec((tf, B_CH), lambda f, b: (b * nf + f, 0))),
        compiler_params=_cparams(("parallel", "parallel")), name="dft_fwd_prod",
    )(ctab, stab, u, hc, hs)


def _dft_inv_kernel(ct_ref, st_ref, yr_ref, yi_ref, u_ref, x0_ref, bias_ref, o_ref, *, inv_l):
    y = (jnp.dot(ct_ref[...], yr_ref[...], preferred_element_type=F32)
         + jnp.dot(st_ref[...], yi_ref[...], preferred_element_type=F32)) * inv_l
    v = y + u_ref[...].astype(F32) * bias_ref[...]
    o_ref[...] = (v * x0_ref[...].astype(F32)).astype(BF16)


def _dft_inv(ctt, stt, yr, yi, u, x0, bias, B, L, tt=512):
    nt = L // tt
    return pl.pallas_call(
        functools.partial(_dft_inv_kernel, inv_l=1.0 / L),
        out_shape=jax.ShapeDtypeStruct((B * L, B_CH), BF16), grid=(nt, B),
        in_specs=[pl.BlockSpec((tt, L), lambda t, b: (t, 0)),
                  pl.BlockSpec((tt, L), lambda t, b: (t, 0)),
                  pl.BlockSpec((L, B_CH), lambda t, b: (b, 0)),
                  pl.BlockSpec((L, B_CH), lambda t, b: (b, 0)),
                  pl.BlockSpec((tt, B_CH), lambda t, b: (b * nt + t, 0)),
                  pl.BlockSpec((tt, B_CH), lambda t, b: (b * nt + t, 0)),
                  pl.BlockSpec((1, B_CH), lambda t, b: (0, 0))],
        out_specs=pl.BlockSpec((tt, B_CH), lambda t, b: (b * nt + t, 0)),
        compiler_params=_cparams(("parallel", "parallel")), name="dft_inv",
    )(ctt, stt, yr, yi, u, x0, bias)


def _hy_positions(L):
    pos = jnp.concatenate([jnp.arange(L), L - jnp.arange(L)]) % L
    t = jnp.linspace(0.0, 1.0, L, dtype=F32)[:, None]
    w = 2.0 * math.pi * jnp.arange(L, dtype=F32)[:, None] / L
    f = jnp.linspace(1e-4, HY_BANDS - 1, HY_BANDS, dtype=F32)[None, :]
    z = jnp.concatenate([t, jnp.cos(f * w), -jnp.sin(f * w)], axis=-1)
    zz = jnp.pad(z[pos], ((0, 0), (0, LANES - HY_EMB))).reshape(2, L, LANES)
    tt = t[pos].reshape(2, L, 1)
    mm = jnp.ones((2, L, 1), F32).at[1, 0, 0].set(0.0)
    return zz, tt, mm


def _hyena(proj, p, tabs, B, L):
    ctab, stab, ctt, stt, zz, tt, mm = tabs
    u, x0 = _hy_prep(proj, p['hy_conv_w'], p['hy_conv_b'][None, :], B, L)
    max_decay = math.log(HY_TARGET) / HY_FAST
    min_decay = math.log(HY_TARGET) / HY_SLOW
    absdelta = jnp.abs(jnp.linspace(min_decay, max_decay, B_CH, dtype=F32))[None, :]
    w1p = jnp.pad(p['hy_w1'], ((0, LANES - HY_EMB), (0, 0)))
    k2 = _hy_filter(zz, tt, mm, w1p, p['hy_b1'][None, :], p['hy_w2'], p['hy_b2'][:, None, :],
                    p['hy_w3'], p['hy_freq'][None, :], absdelta, L)
    hc, hs = _dft_fwd(ctab, stab, k2, L)
    yr, yi = _dft_prod(ctab, stab, u, hc, hs, B, L)
    return _dft_inv(ctt, stt, yr, yi, u, x0, p['hy_bias'][None, :], B, L)


def _na_kernel(var_ref, q_ref, k_ref, v_ref, bias_ref, o_ref, *, rows):
    r = pl.program_id(1)
    wr = min(NA_WR, rows)
    kr0 = jnp.clip(r - wr // 2, 0, rows - wr)
    start = pl.multiple_of(kr0 * GRID_W, GRID_W)
    nk = wr * GRID_W
    q = q_ref[...]
    k = k_ref[pl.ds(start, nk), :]
    v = v_ref[pl.ds(start, nk), :]
    lane = lax.broadcasted_iota(I32, (GRID_W, LANES), 1)
    lane_o = lax.broadcasted_iota(I32, (GRID_W, GROUP_W), 1)
    qi = lax.broadcasted_iota(I32, (GRID_W, nk), 0)
    kc = lax.broadcasted_iota(I32, (GRID_W, nk), 1) % GRID_W
    c0 = jnp.clip(qi - NA_WC // 2, 0, GRID_W - NA_WC)
    ok = (kc >= c0) & (kc < c0 + NA_WC)
    acc = jnp.zeros((GRID_W, GROUP_W), F32)
    for h in range(C_HEADS):
        j, off = h // 2, (h % 2) * HEAD_DIM
        qc = q[:, j * LANES:(j + 1) * LANES]
        qm = jnp.where((lane >= off) & (lane < off + HEAD_DIM), qc, jnp.zeros_like(qc))
        s = lax.dot_general(qm, k[:, j * LANES:(j + 1) * LANES], (((1,), (1,)), ((), ())),
                            preferred_element_type=F32)
        s = jnp.where(ok, s + bias_ref[0, h], NEG)
        m = jnp.max(s, axis=-1, keepdims=True)
        e = jnp.exp2(s - m)
        l = jnp.sum(e, axis=-1, keepdims=True)
        pv = jnp.dot(e.astype(BF16), v, preferred_element_type=F32) * (1.0 / l)
        acc = jnp.where((lane_o >= h * HEAD_DIM) & (lane_o < (h + 1) * HEAD_DIM), pv, acc)
    o_ref[...] = acc.astype(BF16)


def _na_bias(rpb, rows):
    wr = min(NA_WR, rows)
    c = jnp.arange(GRID_W)
    dc = jnp.clip(c[None, :] - c[:, None], 1 - NA_WC, NA_WC - 1) + (NA_WC - 1)
    onehot = (dc[None, :, :] == jnp.arange(2 * NA_WC - 1)[:, None, None]).astype(F32)
    cols = jnp.einsum('hrd,dqk->hrqk', rpb.astype(F32) * LOG2E, onehot,
                      precision=lax.Precision.HIGHEST)
    variants = []
    for i in range(NA_WR):
        blk = cols[:, NA_WR - 1 - i:NA_WR - 1 - i + wr]
        variants.append(jnp.moveaxis(blk, 1, 2).reshape(C_HEADS, GRID_W, wr * GRID_W))
    return jnp.stack(variants)


def _na_attn(proj, bias, B, L):
    rows = L // GRID_W
    wr = min(NA_WR, rows)
    r = jnp.arange(rows)
    variant = (r - jnp.clip(r - wr // 2, 0, rows - wr)).astype(I32)
    nk = wr * GRID_W
    grid_spec = pltpu.PrefetchScalarGridSpec(
        num_scalar_prefetch=1, grid=(B, rows),
        in_specs=[pl.BlockSpec((GRID_W, GROUP_W), lambda b, r, var: (b * rows + r, COL_CQ)),
                  pl.BlockSpec((L, GROUP_W), lambda b, r, var: (b, COL_CK)),
                  pl.BlockSpec((L, GROUP_W), lambda b, r, var: (b, COL_CV)),
                  pl.BlockSpec((1, C_HEADS, GRID_W, nk), lambda b, r, var: (var[r], 0, 0, 0))],
        out_specs=pl.BlockSpec((GRID_W, GROUP_W), lambda b, r, var: (b * rows + r, 0)))
    return pl.pallas_call(
        functools.partial(_na_kernel, rows=rows),
        out_shape=jax.ShapeDtypeStruct((B * L, GROUP_W), BF16),
        grid_spec=grid_spec,
        compiler_params=_cparams(("parallel", "arbitrary")),
        name="na_attn",
    )(variant, proj, proj, proj, bias)


WQ_TILE = 256
WK_SPAN = WQ_TILE + 2 * WIN


def _win_kernel(sink_ref, q_ref, k_ref, v_ref, o_ref, *, L):
    i = pl.program_id(1)
    q0 = i * WQ_TILE
    k0 = pl.multiple_of(jnp.clip(q0 - WIN, 0, L - WK_SPAN), WIN)
    k = k_ref[pl.ds(k0, WK_SPAN), :]
    v = v_ref[pl.ds(k0, WK_SPAN), :]
    qf = q_ref[...].astype(F32)
    qpos = q0 + lax.broadcasted_iota(I32, (WQ_TILE, WK_SPAN), 0)
    kpos = k0 + lax.broadcasted_iota(I32, (WQ_TILE, WK_SPAN), 1)
    ok = jnp.abs(qpos - kpos) <= WIN
    lane = lax.broadcasted_iota(I32, (WQ_TILE, LANES), 1)
    for j in range(2):
        out = jnp.zeros((WQ_TILE, LANES), F32)
        for hh in range(2):
            h = 2 * j + hh
            g = h // (D_HEADS // D_KV_HEADS)
            qc = qf[:, j * LANES:(j + 1) * LANES]
            if hh != g:
                qc = pltpu.roll(qc, HEAD_DIM, 1)
            in_g = (lane >= g * HEAD_DIM) & (lane < (g + 1) * HEAD_DIM)
            qm = jnp.where(in_g, qc, 0.0).astype(BF16)
            s = lax.dot_general(qm, k, (((1,), (1,)), ((), ())), preferred_element_type=F32)
            s = jnp.where(ok, s, NEG)
            sk = sink_ref[h] * LOG2E
            m = jnp.maximum(jnp.max(s, axis=-1, keepdims=True), sk)
            e = jnp.exp2(s - m)
            l = jnp.sum(e, axis=-1, keepdims=True) + jnp.exp2(sk - m)
            pv = jnp.dot(e.astype(BF16), v, preferred_element_type=F32) * (1.0 / l)
            pv = jnp.where(in_g, pv, 0.0)
            if hh != g:
                pv = pltpu.roll(pv, HEAD_DIM, 1)
            out = out + pv
        o_ref[:, j * LANES:(j + 1) * LANES] = out.astype(BF16)


def _win_attn(proj, sink, B, L):
    nq = L // WQ_TILE
    return pl.pallas_call(
        functools.partial(_win_kernel, L=L),
        out_shape=jax.ShapeDtypeStruct((B * L, GROUP_W), BF16),
        grid=(B, nq),
        in_specs=[pl.BlockSpec(memory_space=pltpu.SMEM),
                  pl.BlockSpec((WQ_TILE, GROUP_W), lambda b, i: (b * nq + i, COL_DQ)),
                  pl.BlockSpec((L, LANES), lambda b, i: (b, COL_DK)),
                  pl.BlockSpec((L, LANES), lambda b, i: (b, COL_DV))],
        out_specs=pl.BlockSpec((WQ_TILE, GROUP_W), lambda b, i: (b * nq + i, 0)),
        compiler_params=_cparams(("parallel", "parallel")),
        name="win_attn",
    )(sink, proj, proj, proj)


def _layer_norm(y, g, b):
    mu = jnp.mean(y, axis=-1, keepdims=True)
    d = y - mu
    var = jnp.mean(d * d, axis=-1, keepdims=True)
    return d * lax.rsqrt(var + LN_EPS) * g + b


PACK_W = D_MODEL // 2
HI_MASK = -65536


def _pack_bf16_pairs(y):
    lo = lax.bitcast_convert_type(y[:, :PACK_W].astype(BF16).astype(F32), I32)
    hi = lax.bitcast_convert_type(y[:, PACK_W:].astype(BF16).astype(F32), I32)
    return lax.shift_right_logical(lo, 16) | (hi & HI_MASK)


def _unpack_bf16_pairs(w):
    return (lax.bitcast_convert_type(w << 16, F32), lax.bitcast_convert_type(w & HI_MASK, F32))


def _out_proj_kernel(oa_ref, ob_ref, oc_ref, od_ref, w_ref, x_ref, g_ref, b_ref, o_ref, p_ref):
    acc = DN_ALPHA * x_ref[...]
    for gi, ref in enumerate((oa_ref, ob_ref, oc_ref, od_ref)):
        acc = acc + jnp.dot(ref[...], w_ref[gi * GROUP_W:(gi + 1) * GROUP_W, :],
                            preferred_element_type=F32)
    y = _layer_norm(acc, g_ref[...], b_ref[...])
    o_ref[...] = y
    p_ref[...] = _pack_bf16_pairs(y)


def _out_proj(oa, ob, oc, od, w_bf, x, g, b, tm=512):
    T = x.shape[0]
    og = pl.BlockSpec((tm, GROUP_W), lambda i: (i, 0))
    row = pl.BlockSpec((tm, D_MODEL), lambda i: (i, 0))
    vec = pl.BlockSpec((1, D_MODEL), lambda i: (0, 0))
    return pl.pallas_call(
        _out_proj_kernel,
        out_shape=(jax.ShapeDtypeStruct((T, D_MODEL), F32), jax.ShapeDtypeStruct((T, PACK_W), I32)),
        grid=(T // tm,),
        in_specs=[og, og, og, og, pl.BlockSpec((D_MODEL, D_MODEL), lambda i: (0, 0)), row, vec, vec],
        out_specs=(row, pl.BlockSpec((tm, PACK_W), lambda i: (i, 0))),
        compiler_params=_cparams(("parallel",)),
        name="out_proj_ln",
    )(oa, ob, oc, od, w_bf, x, g, b)


RT_TM = 256


def _router_kernel(x_ref, wh_ref, wl_ref, bias_ref, tri_ref, te_ref, gate_ref, rank_ref, cnt_ref,
                   carry):
    @pl.when(pl.program_id(0) == 0)
    def _():
        carry[...] = jnp.zeros_like(carry)

    x = x_ref[...]
    xh = x.astype(BF16)
    xl = (x - xh.astype(F32)).astype(BF16)
    logits = (jnp.dot(xh, wh_ref[...], preferred_element_type=F32)
              + jnp.dot(xh, wl_ref[...], preferred_element_type=F32)
              + jnp.dot(xl, wh_ref[...], preferred_element_type=F32))
    scores = 1.0 / (1.0 + jnp.exp(-logits))
    sel = scores + bias_ref[...]
    lane_e = lax.broadcasted_iota(I32, (RT_TM, N_EXPERTS), 1).astype(F32)
    lane_k = lax.broadcasted_iota(I32, (RT_TM, LANES), 1)
    te = jnp.zeros((RT_TM, LANES), F32)
    ts = jnp.zeros((RT_TM, LANES), F32)
    onehot = jnp.zeros((RT_TM, N_EXPERTS), F32)
    for k in range(TOP_K):
        m = jnp.max(sel, axis=-1, keepdims=True)
        idx = jnp.min(jnp.where(sel == m, lane_e, float(N_EXPERTS)), axis=-1, keepdims=True)
        hit = lane_e == idx
        sc = jnp.sum(jnp.where(hit, scores, 0.0), axis=-1, keepdims=True)
        te = jnp.where(lane_k == k, idx, te)
        ts = jnp.where(lane_k == k, sc, ts)
        onehot = jnp.where(hit, 1.0, onehot)
        sel = jnp.where(hit, -jnp.inf, sel)
    gate_ref[...] = ts / jnp.sum(ts, axis=-1, keepdims=True) * ROUTED_SCALE
    te_ref[...] = te.astype(I32)
    before = jnp.dot(tri_ref[...], onehot.astype(BF16), preferred_element_type=F32) + carry[...]
    rank = jnp.zeros((RT_TM, LANES), F32)
    for k in range(TOP_K):
        rk = jnp.sum(jnp.where(lane_e == te[:, k:k + 1], before, 0.0), axis=-1, keepdims=True)
        rank = jnp.where(lane_k == k, rk, rank)
    rank_ref[...] = rank.astype(I32)
    carry[...] = carry[...] + jnp.sum(onehot, axis=0, keepdims=True)
    cnt_ref[...] = carry[...].astype(I32)


def _router(x1, wr, bias):
    T = x1.shape[0]
    wh = wr.astype(BF16)
    wl = (wr - wh.astype(F32)).astype(BF16)
    tri = (jnp.arange(RT_TM)[:, None] > jnp.arange(RT_TM)[None, :]).astype(BF16)
    row = pl.BlockSpec((RT_TM, LANES), lambda i: (i, 0))
    full = lambda shape: pl.BlockSpec(shape, lambda i: (0, 0))
    return pl.pallas_call(
        _router_kernel,
        out_shape=(jax.ShapeDtypeStruct((T, LANES), I32), jax.ShapeDtypeStruct((T, LANES), F32),
                   jax.ShapeDtypeStruct((T, LANES), I32), jax.ShapeDtypeStruct((1, N_EXPERTS), I32)),
        grid=(T // RT_TM,),
        in_specs=[pl.BlockSpec((RT_TM, D_MODEL), lambda i: (i, 0)),
                  full((D_MODEL, N_EXPERTS)), full((D_MODEL, N_EXPERTS)), full((1, N_EXPERTS)),
                  full((RT_TM, RT_TM))],
        out_specs=(row, row, row, full((1, N_EXPERTS))),
        scratch_shapes=[pltpu.VMEM((1, N_EXPERTS), F32)],
        compiler_params=_cparams(("arbitrary",)),
        name="router_topk",
    )(x1, wh, wl, bias[None, :], tri)


def _idx_copy(idx_hbm, idx_smem, isem, blk, slot):
    return pltpu.make_async_copy(idx_hbm.at[blk], idx_smem.at[slot], isem.at[slot])


ROW_SUB = D_MODEL // LANES


def _row_copy(src_hbm, buf, sem, idx_smem, slot, r):
    src = pl.multiple_of(idx_smem[slot, r] * ROW_SUB, ROW_SUB)
    return pltpu.make_async_copy(src_hbm.at[pl.ds(src, ROW_SUB), :],
                                 buf.at[slot, pl.ds(r * ROW_SUB, ROW_SUB), :], sem.at[slot])


def _tile_rows_chunk(buf_slot, row0, n_rows, c):
    return buf_slot[pl.ds(row0 * ROW_SUB + c, n_rows, stride=ROW_SUB), :]


def _gather_step(i, n, idx_hbm, src_hbm, idx_smem, buf, isem, sem, n_rows):
    def issue_rows(slot):
        for r in range(n_rows):
            _row_copy(src_hbm, buf, sem, idx_smem, slot, r).start()

    @pl.when(i == 0)
    def _():
        _idx_copy(idx_hbm, idx_smem, isem, 0, 0).start()
        _idx_copy(idx_hbm, idx_smem, isem, 0, 0).wait()
        issue_rows(0)

        @pl.when(n > 1)
        def _():
            _idx_copy(idx_hbm, idx_smem, isem, 1, 1).start()

    nxt = (i + 1) % 2
    cur = i % 2

    @pl.when(i + 1 < n)
    def _():
        _idx_copy(idx_hbm, idx_smem, isem, i + 1, nxt).wait()
        issue_rows(nxt)

    for r in range(n_rows):
        _row_copy(src_hbm, buf, sem, idx_smem, cur, r).wait()

    @pl.when(i + 2 < n)
    def _():
        _idx_copy(idx_hbm, idx_smem, isem, i + 2, cur).start()

    return cur


PACK_SUB = PACK_W // LANES
EXP_VMEM_LIMIT = 56 * 1024 * 1024


def _expert_kernel(blk_e_ref, nused_ref, tok_hbm, xp_hbm, wg_ref, wu_ref, wd_ref, y_ref,
                   idx_smem, xp_vmem, gbuf, wgb, wub, wdb, isem, xsem):
    i = pl.program_id(0)
    n = pl.num_programs(0)

    def idx_copy(blk, slot):
        return pltpu.make_async_copy(tok_hbm.at[blk], idx_smem.at[slot], isem.at[slot])

    @pl.when(i == 0)
    def _():
        load_all = pltpu.make_async_copy(xp_hbm, xp_vmem, xsem)
        load_all.start()
        idx_copy(0, 0).start()
        load_all.wait()

    cur = i % 2
    idx_copy(i, cur).wait()

    @pl.when(i + 1 < n)
    def _():
        idx_copy(i + 1, 1 - cur).start()

    prev_e = blk_e_ref[jnp.maximum(i - 1, 0)]

    @pl.when((i == 0) | (blk_e_ref[i] != prev_e))
    def _():
        wgb[...] = wg_ref[0, 0].astype(BF16)
        wub[...] = wu_ref[0, 0].astype(BF16)
        wdb[...] = wd_ref[0, 0].astype(BF16)

    @pl.when(i < nused_ref[0])
    def _():
        for r in range(EXP_BLK):
            gbuf[pl.ds(r, PACK_SUB, stride=EXP_BLK), :] = xp_vmem[idx_smem[cur, r]]
        halves = [_unpack_bf16_pairs(gbuf[c * EXP_BLK:(c + 1) * EXP_BLK, :]) for c in range(PACK_SUB)]
        xb = jnp.concatenate([h[0] for h in halves] + [h[1] for h in halves], axis=1).astype(BF16)
        g = jnp.dot(xb, wgb[...], preferred_element_type=F32)
        u = jnp.dot(xb, wub[...], preferred_element_type=F32)
        h = (g * (1.0 / (1.0 + jnp.exp(-g))) * u).astype(BF16)
        y = jnp.dot(h, wdb[...], preferred_element_type=F32)
        for c in range(ROW_SUB):
            y_ref[pl.ds(c, EXP_BLK, stride=ROW_SUB), :] = y[:, c * LANES:(c + 1) * LANES]

    @pl.when(i >= nused_ref[0])
    def _():
        y_ref[...] = jnp.zeros_like(y_ref)


def _experts(xp, row_tok, blk_e, n_used, e_gate, e_up, e_down, layer):
    nblk = row_tok.shape[0]
    wmap = lambda i, be, nu: (layer, be[i], 0, 0)
    grid_spec = pltpu.PrefetchScalarGridSpec(
        num_scalar_prefetch=2, grid=(nblk,),
        in_specs=[pl.BlockSpec(memory_space=pl.ANY),
                  pl.BlockSpec(memory_space=pl.ANY),
                  pl.BlockSpec((1, 1, D_MODEL, D_EXPERT), wmap),
                  pl.BlockSpec((1, 1, D_MODEL, D_EXPERT), wmap),
                  pl.BlockSpec((1, 1, D_EXPERT, D_MODEL), wmap)],
        out_specs=pl.BlockSpec((EXP_BLK * ROW_SUB, LANES), lambda i, be, nu: (i, 0)),
        scratch_shapes=[pltpu.SMEM((2, EXP_BLK), I32),
                        pltpu.VMEM(xp.shape, I32),
                        pltpu.VMEM((PACK_SUB * EXP_BLK, LANES), I32),
                        pltpu.VMEM((D_MODEL, D_EXPERT), BF16),
                        pltpu.VMEM((D_MODEL, D_EXPERT), BF16),
                        pltpu.VMEM((D_EXPERT, D_MODEL), BF16),
                        pltpu.SemaphoreType.DMA((2,)),
                        pltpu.SemaphoreType.DMA])
    return pl.pallas_call(
        _expert_kernel,
        out_shape=jax.ShapeDtypeStruct((nblk * EXP_BLK * ROW_SUB, LANES), F32),
        grid_spec=grid_spec,
        compiler_params=pltpu.CompilerParams(dimension_semantics=("arbitrary",),
                                             vmem_limit_bytes=EXP_VMEM_LIMIT),
        name="moe_experts",
    )(blk_e, n_used, row_tok, xp, e_gate, e_up, e_down)


def _combine_kernel(dest_hbm, ys_hbm, x_ref, gate_ref, sg_ref, su_ref, sd_ref, g_ref, b_ref, o_ref,
                    idx_smem, buf, isem, sem):
    i = pl.program_id(0)
    n = pl.num_programs(0)
    slot = _gather_step(i, n, dest_hbm, ys_hbm, idx_smem, buf, isem, sem, CMB_TOK * TOP_K)
    x = x_ref[...]
    xb = x.astype(BF16)
    g = jnp.dot(xb, sg_ref[...], preferred_element_type=F32)
    u = jnp.dot(xb, su_ref[...], preferred_element_type=F32)
    h = (g * (1.0 / (1.0 + jnp.exp(-g))) * u).astype(BF16)
    acc = DN_ALPHA * x + jnp.dot(h, sd_ref[...], preferred_element_type=F32)
    gate = gate_ref[...]
    chunks = []
    for c in range(ROW_SUB):
        part = jnp.zeros((CMB_TOK, LANES), F32)
        for k in range(TOP_K):
            part = part + _tile_rows_chunk(buf.at[slot], k * CMB_TOK, CMB_TOK, c) * gate[:, k:k + 1]
        chunks.append(part)
    acc = acc + jnp.concatenate(chunks, axis=1)
    o_ref[...] = _layer_norm(acc, g_ref[...], b_ref[...])


def _combine(ys, dest_km, x1, gates, sg, su, sd, g, b):
    T = x1.shape[0]
    full = lambda shape: pl.BlockSpec(shape, lambda i: (0, 0))
    row = pl.BlockSpec((CMB_TOK, D_MODEL), lambda i: (i, 0))
    return pl.pallas_call(
        _combine_kernel,
        out_shape=jax.ShapeDtypeStruct((T, D_MODEL), F32),
        grid=(T // CMB_TOK,),
        in_specs=[pl.BlockSpec(memory_space=pl.ANY), pl.BlockSpec(memory_space=pl.ANY),
                  row, pl.BlockSpec((CMB_TOK, LANES), lambda i: (i, 0)),
                  full((D_MODEL, D_SHARED)), full((D_MODEL, D_SHARED)), full((D_SHARED, D_MODEL)),
                  full((1, D_MODEL)), full((1, D_MODEL))],
        out_specs=row,
        scratch_shapes=[pltpu.SMEM((2, CMB_TOK * TOP_K), I32),
                        pltpu.VMEM((2, CMB_TOK * TOP_K * ROW_SUB, LANES), F32),
                        pltpu.SemaphoreType.DMA((2,)),
                        pltpu.SemaphoreType.DMA((2,))],
        compiler_params=_cparams(("arbitrary",)),
        name="moe_combine_ln",
    )(dest_km, ys, x1, gates, sg, su, sd, g, b)


DEST_TM = 1024


def _dest_kernel(te_ref, rank_ref, ps_ref, d_ref):
    te = te_ref[...]
    lane_e = lax.broadcasted_iota(I32, (DEST_TM, N_EXPERTS), 1)
    lane_k = lax.broadcasted_iota(I32, (DEST_TM, LANES), 1)
    dest = jnp.zeros((DEST_TM, LANES), F32)
    for k in range(TOP_K):
        v = jnp.sum(jnp.where(lane_e == te[:, k:k + 1], ps_ref[...], 0.0), axis=-1, keepdims=True)
        dest = jnp.where(lane_k == k, v, dest)
    d_ref[...] = dest.astype(I32) + rank_ref[...]


def _dest_rows(te, rank, pstart):
    T = te.shape[0]
    row = pl.BlockSpec((DEST_TM, LANES), lambda i: (i, 0))
    return pl.pallas_call(
        _dest_kernel, out_shape=jax.ShapeDtypeStruct((T, LANES), I32), grid=(T // DEST_TM,),
        in_specs=[row, row, pl.BlockSpec((1, N_EXPERTS), lambda i: (0, 0))], out_specs=row,
        compiler_params=_cparams(("parallel",)), name="moe_dest",
    )(te, rank, pstart)


def _moe_ln(x1, xp, p, stacked, layer):
    T = x1.shape[0]
    te, gates, rank, counts = _router(x1, p['router_w'], p['router_bias'])
    counts = counts[0]
    padded = (counts + EXP_BLK - 1) // EXP_BLK * EXP_BLK
    pends = jnp.cumsum(padded)
    dest = _dest_rows(te, rank, (pends - padded).astype(F32)[None, :])[:, :TOP_K]
    nblk = (T * TOP_K + N_EXPERTS * (EXP_BLK - 1)) // EXP_BLK + 1
    tok = jnp.broadcast_to(jnp.arange(T, dtype=I32)[:, None], (T, TOP_K))
    row_tok = jnp.zeros((nblk * EXP_BLK,), I32).at[dest.reshape(-1)].set(
        tok.reshape(-1), unique_indices=True, mode='promise_in_bounds')
    blk_start = jnp.arange(nblk, dtype=I32) * EXP_BLK
    blk_e = jnp.minimum(jnp.sum((pends[None, :] <= blk_start[:, None]).astype(I32), axis=1),
                        N_EXPERTS - 1)
    n_used = (pends[-1] // EXP_BLK).astype(I32)[None]
    ys = _experts(xp.reshape(T, PACK_SUB, LANES), row_tok.reshape(nblk, EXP_BLK), blk_e, n_used,
                  stacked['e_gate'], stacked['e_up'], stacked['e_down'], layer)
    dest_km = dest.reshape(T // CMB_TOK, CMB_TOK, TOP_K).transpose(0, 2, 1).reshape(
        T // CMB_TOK, CMB_TOK * TOP_K)
    return _combine(ys, dest_km, x1, gates, p['s_gate'].astype(BF16), p['s_up'].astype(BF16),
                    p['s_down'].astype(BF16), p['ln2_g'][None, :], p['ln2_b'][None, :])


STACKED = ('e_gate', 'e_up', 'e_down')


def _encoder_layer(x, p, stacked, layer, lam_init, consts, B, L):
    ca, sa, cd, sd, gmat, hy_tabs = consts
    proj = _in_proj(x, p['w_in'].astype(BF16), ca, sa, cd, sd, L)
    lam = (jnp.exp(jnp.sum(p['lam_q1'] * p['lam_k1'])) - jnp.exp(jnp.sum(p['lam_q2'] * p['lam_k2']))
           + lam_init).astype(F32)[None]
    gn = (jnp.tile(p['a_subln_g'], A_HEADS) * (1.0 - lam_init))[None, :]
    o_a = _diff_attn(proj, lam, gn, gmat, B, L)
    o_b = _hyena(proj, p, hy_tabs, B, L)
    o_c = _na_attn(proj, _na_bias(p['na_rpb'], L // GRID_W), B, L)
    o_d = _win_attn(proj, p['d_sink'], B, L)
    x1, xp = _out_proj(o_a, o_b, o_c, o_d, p['w_out'].astype(BF16), x,
                       p['ln1_g'][None, :], p['ln1_b'][None, :])
    return _moe_ln(x1, xp, p, stacked, layer)


def _trunk(x3, params):
    B, L, _ = x3.shape
    ca, sa = _rope_tables(L, A_DH)
    cd, sd = _rope_tables(L, HEAD_DIM)
    head = jnp.arange(GROUP_W) // HEAD_DIM
    gmat = ((head[:, None] == head[None, :]).astype(F32) / HEAD_DIM).astype(BF16)
    hy_tabs = _trig_tables(L, False) + _trig_tables(L, True) + _hy_positions(L)
    consts = (ca, sa, cd, sd, gmat, hy_tabs)
    x = x3.reshape(B * L, D_MODEL)
    stacked = {name: params[name] for name in STACKED}
    for l in range(DEPTH):
        lp = {name: arr[l] for name, arr in params.items() if name not in STACKED}
        x = _encoder_layer(x, lp, stacked, l, 0.8 - 0.6 * math.exp(-0.3 * l), consts, B, L)
    return x.reshape(B, L, D_MODEL)


def kernel(x_prompt, x_sample, w_in, w_out, ln1_g, ln1_b, ln2_g, ln2_b, lam_q1, lam_k1, lam_q2,
           lam_k2, a_subln_g, hy_conv_w, hy_conv_b, hy_w1, hy_b1, hy_w2, hy_b2, hy_w3, hy_freq,
           hy_bias, na_rpb, d_sink, router_w, router_bias, e_gate, e_up, e_down, s_gate, s_up,
           s_down):
    params = {
        'w_in': w_in, 'w_out': w_out, 'ln1_g': ln1_g, 'ln1_b': ln1_b, 'ln2_g': ln2_g, 'ln2_b': ln2_b,
        'lam_q1': lam_q1, 'lam_k1': lam_k1, 'lam_q2': lam_q2, 'lam_k2': lam_k2,
        'a_subln_g': a_subln_g, 'hy_conv_w': hy_conv_w, 'hy_conv_b': hy_conv_b, 'hy_w1': hy_w1,
        'hy_b1': hy_b1, 'hy_w2': hy_w2, 'hy_b2': hy_b2, 'hy_w3': hy_w3, 'hy_freq': hy_freq,
        'hy_bias': hy_bias, 'na_rpb': na_rpb, 'd_sink': d_sink, 'router_w': router_w,
        'router_bias': router_bias, 'e_gate': e_gate, 'e_up': e_up, 'e_down': e_down,
        's_gate': s_gate, 's_up': s_up, 's_down': s_down,
    }
    return _trunk(x_prompt, params), _trunk(x_sample, params)
```

```python
import functools
import math

import jax
import jax.numpy as jnp
from jax import lax
from jax.experimental import pallas as pl
from jax.experimental.pallas import tpu as pltpu

F32 = jnp.float32
BF16 = jnp.bfloat16
I32 = jnp.int32

D_MODEL = 1024
DEPTH = 2
GRID_W = 64
HEAD_DIM = 64
GROUP_W = 256
A_HEADS = 4
A_DH = 32
B_CH = 256
HY_EMB = 33
HY_BANDS = 16
HY_FFN = 64
HY_INNER = 2
HY_TARGET = 1e-2
HY_FAST = 0.3
HY_SLOW = 1.5
C_HEADS = 4
NA_WR = 8
NA_WC = 16
D_HEADS = 4
D_KV_HEADS = 2
WIN = 128
D_IN = 2816
N_EXPERTS = 256
TOP_K = 8
D_EXPERT = 256
D_SHARED = 256
ROUTED_SCALE = 2.5
ROPE_THETA = 10000.0
LN_EPS = 1e-5
DN_ALPHA = (2 * DEPTH) ** 0.25
NEG = -1e30
LOG2E = 1.4426950408889634

LANES = 128
VMEM_LIMIT = 48 * 1024 * 1024

COL_AQ, COL_AK, COL_AV = 0, 1, 2
COL_B = 1
COL_CQ, COL_CK, COL_CV = 6, 7, 8
COL_DQ = 9
COL_DK, COL_DV = 20, 21

EXP_BLK = 256
CMB_TOK = 128


def _cparams(sem):
    return pltpu.CompilerParams(dimension_semantics=sem, vmem_limit_bytes=VMEM_LIMIT)


def _rope_lanes(y, cos, sin_signed, half):
    lane = lax.broadcasted_iota(I32, y.shape, 1)
    first = (lane % (2 * half)) < half
    rot = jnp.where(first, pltpu.roll(y, LANES - half, 1), pltpu.roll(y, half, 1))
    return y * cos + rot * sin_signed


def _in_proj_kernel(x_ref, w_ref, ca_ref, sa_ref, cd_ref, sd_ref, o_ref):
    xb = x_ref[...].astype(BF16)
    ca, sa, cd, sd = ca_ref[...], sa_ref[...], cd_ref[...], sd_ref[...]
    a_scale = (A_DH ** -0.5) * LOG2E
    hd_scale = (HEAD_DIM ** -0.5) * LOG2E
    for c in range(D_IN // 256):
        y = jnp.dot(xb, w_ref[:, c * 256:(c + 1) * 256], preferred_element_type=F32)
        for hh in range(2):
            ch = 2 * c + hh
            z = y[:, hh * LANES:(hh + 1) * LANES]
            if ch in (0, 1):
                z = _rope_lanes(z, ca, sa, A_DH // 2) * a_scale
            elif ch in (2, 3):
                z = _rope_lanes(z, ca, sa, A_DH // 2)
            elif ch in (12, 13):
                z = z * hd_scale
            elif ch in (18, 19):
                z = _rope_lanes(z, cd, sd, HEAD_DIM // 2) * hd_scale
            elif ch == 20:
                z = _rope_lanes(z, cd, sd, HEAD_DIM // 2)
            o_ref[:, ch * LANES:(ch + 1) * LANES] = z.astype(BF16)


def _in_proj(x, w_bf, ca, sa, cd, sd, L, tm=512):
    T = x.shape[0]
    nl = L // tm
    tab = pl.BlockSpec((tm, LANES), lambda i: (i % nl, 0))
    return pl.pallas_call(
        _in_proj_kernel,
        out_shape=jax.ShapeDtypeStruct((T, D_IN), BF16),
        grid=(T // tm,),
        in_specs=[pl.BlockSpec((tm, D_MODEL), lambda i: (i, 0)),
                  pl.BlockSpec((D_MODEL, D_IN), lambda i: (0, 0)),
                  tab, tab, tab, tab],
        out_specs=pl.BlockSpec((tm, D_IN), lambda i: (i, 0)),
        compiler_params=_cparams(("parallel",)),
        name="in_proj",
    )(x, w_bf, ca, sa, cd, sd)


def _rope_tables(L, dh):
    half = dh // 2
    lane = jnp.arange(LANES)
    inv_freq = ROPE_THETA ** (-(2.0 * (lane % half)).astype(F32) / dh)
    ang = jnp.arange(L, dtype=F32)[:, None] * inv_freq[None, :]
    sign = jnp.where((lane % dh) < half, -1.0, 1.0).astype(F32)
    return jnp.cos(ang), jnp.sin(ang) * sign[None, :]


def _diff_attn_kernel(lam_ref, q_ref, k_ref, v_ref, gn_ref, gm_ref, o_ref):
    lam = lam_ref[0]
    tq = q_ref.shape[0]
    q = q_ref[...]
    v = v_ref[...]
    lane = lax.broadcasted_iota(I32, (tq, LANES), 1)
    lane_o = lax.broadcasted_iota(I32, (tq, GROUP_W), 1)
    acc = jnp.zeros((tq, GROUP_W), F32)
    for h in range(A_HEADS):
        parts = []
        for c in range(2):
            comp = 2 * h + c
            j, off = comp // 4, (comp % 4) * A_DH
            qc = q[:, j * LANES:(j + 1) * LANES]
            qm = jnp.where((lane >= off) & (lane < off + A_DH), qc, jnp.zeros_like(qc))
            s = lax.dot_general(qm, k_ref[:, j * LANES:(j + 1) * LANES],
                                (((1,), (1,)), ((), ())), preferred_element_type=F32)
            m = jnp.max(s, axis=-1, keepdims=True)
            e = jnp.exp2(s - m)
            l = jnp.sum(e, axis=-1, keepdims=True)
            pv = jnp.dot(e.astype(BF16), v, preferred_element_type=F32)
            parts.append(pv * (1.0 / l))
        o_h = parts[0] - lam * parts[1]
        acc = jnp.where((lane_o >= h * HEAD_DIM) & (lane_o < (h + 1) * HEAD_DIM), o_h, acc)
    sq = acc * acc
    hi = sq.astype(BF16)
    lo = (sq - hi.astype(F32)).astype(BF16)
    ms = (jnp.dot(hi, gm_ref[...], preferred_element_type=F32)
          + jnp.dot(lo, gm_ref[...], preferred_element_type=F32))
    o_ref[...] = (acc * lax.rsqrt(ms + LN_EPS) * gn_ref[...]).astype(BF16)


def _diff_attn(proj, lam, gn, gmat, B, L, tq=256):
    nq = L // tq
    return pl.pallas_call(
        _diff_attn_kernel,
        out_shape=jax.ShapeDtypeStruct((B * L, GROUP_W), BF16),
        grid=(B, nq),
        in_specs=[pl.BlockSpec(memory_space=pltpu.SMEM),
                  pl.BlockSpec((tq, GROUP_W), lambda b, i: (b * nq + i, COL_AQ)),
                  pl.BlockSpec((L, GROUP_W), lambda b, i: (b, COL_AK)),
                  pl.BlockSpec((L, GROUP_W), lambda b, i: (b, COL_AV)),
                  pl.BlockSpec((1, GROUP_W), lambda b, i: (0, 0)),
                  pl.BlockSpec((GROUP_W, GROUP_W), lambda b, i: (0, 0))],
        out_specs=pl.BlockSpec((tq, GROUP_W), lambda b, i: (b * nq + i, 0)),
        compiler_params=_cparams(("parallel", "parallel")),
        name="diff_attn",
    )(lam, proj, proj, proj, gn, gmat)


HY_RC = 512


def _hy_prep_kernel(p_ref, w_ref, b_ref, u_ref, x0_ref):
    L = p_ref.shape[0]
    w0, w1, w2, bias = w_ref[0:1, :], w_ref[1:2, :], w_ref[2:3, :], b_ref[...]
    row = lax.broadcasted_iota(I32, (HY_RC, 3 * B_CH), 0)
    zero_row = jnp.zeros((1, 3 * B_CH), F32)
    for r in range(L // HY_RC):
        r0 = r * HY_RC
        xf = p_ref[r0:r0 + HY_RC, :].astype(F32)
        prev_row = zero_row if r == 0 else p_ref[r0 - 8:r0, :].astype(F32)[7:8, :]
        next_row = (zero_row if r0 + HY_RC == L
                    else p_ref[r0 + HY_RC:r0 + HY_RC + 8, :].astype(F32)[0:1, :])
        xp = jnp.where(row == 0, prev_row, pltpu.roll(xf, 1, 0))
        xn = jnp.where(row == HY_RC - 1, next_row, pltpu.roll(xf, HY_RC - 1, 0))
        y = bias + xp * w0 + xf * w1 + xn * w2
        x0, x1, v = y[:, :B_CH], y[:, B_CH:2 * B_CH], y[:, 2 * B_CH:]
        u_ref[r0:r0 + HY_RC, :] = (v * x1).astype(BF16)
        x0_ref[r0:r0 + HY_RC, :] = x0.astype(BF16)


def _hy_prep(proj, conv_w, conv_b, B, L):
    out = jax.ShapeDtypeStruct((B * L, B_CH), BF16)
    return pl.pallas_call(
        _hy_prep_kernel,
        out_shape=(out, out),
        grid=(B,),
        in_specs=[pl.BlockSpec((L, 3 * B_CH), lambda b: (b, COL_B)),
                  pl.BlockSpec((3, 3 * B_CH), lambda b: (0, 0)),
                  pl.BlockSpec((1, 3 * B_CH), lambda b: (0, 0))],
        out_specs=(pl.BlockSpec((L, B_CH), lambda b: (b, 0)),
                   pl.BlockSpec((L, B_CH), lambda b: (b, 0))),
        compiler_params=_cparams(("parallel",)),
        name="hy_prep",
    )(proj, conv_w, conv_b)


def _trig_table_kernel(ar_ref, ai_ref, br_ref, bi_ref, c_ref, s_ref):
    br, bi = br_ref[...], bi_ref[...]
    for th in range(c_ref.shape[1] // LANES):
        ar, ai = ar_ref[:, th:th + 1], ai_ref[:, th:th + 1]
        c_ref[:, th * LANES:(th + 1) * LANES] = (ar * br - ai * bi).astype(BF16)
        s_ref[:, th * LANES:(th + 1) * LANES] = (ar * bi + ai * br).astype(BF16)


def _trig_tables(L, transposed, tr=512):
    r = jnp.arange(L, dtype=I32)[:, None]
    hi = jnp.arange(L // LANES, dtype=I32)[None, :]
    lo = jnp.arange(LANES, dtype=I32)[None, :]
    if transposed:
        pa, pb = (2 * LANES * hi * r) % (4 * L), ((2 * lo + 1) * r) % (4 * L)
    else:
        pa, pb = ((2 * r + 1) * LANES * hi) % (4 * L), ((2 * r + 1) * lo) % (4 * L)
    ang_a = pa.astype(F32) * (math.pi / (2 * L))
    ang_b = pb.astype(F32) * (math.pi / (2 * L))
    out = jax.ShapeDtypeStruct((L, L), BF16)
    sa = pl.BlockSpec((tr, L // LANES), lambda i: (i, 0))
    sb = pl.BlockSpec((tr, LANES), lambda i: (i, 0))
    so = pl.BlockSpec((tr, L), lambda i: (i, 0))
    return pl.pallas_call(
        _trig_table_kernel, out_shape=(out, out), grid=(L // tr,),
        in_specs=[sa, sa, sb, sb], out_specs=(so, so),
        compiler_params=_cparams(("parallel",)), name="trig_tables",
    )(jnp.cos(ang_a), jnp.sin(ang_a), jnp.cos(ang_b), jnp.sin(ang_b))


def _hy_filter_kernel(z_ref, t_ref, m_ref, w1_ref, b1_ref, w2_ref, b2_ref, w3_ref, fr_ref,
                      ad_ref, k_ref):
    hp = lax.Precision.HIGHEST
    fr = fr_ref[...]
    h = jnp.sin(fr * (jnp.dot(z_ref[0], w1_ref[...], precision=hp, preferred_element_type=F32)
                      + b1_ref[...]))
    for i in range(HY_INNER):
        h = jnp.sin(fr * (jnp.dot(h, w2_ref[i], precision=hp, preferred_element_type=F32)
                          + b2_ref[i]))
    h = jnp.dot(h, w3_ref[...], precision=hp, preferred_element_type=F32)
    decay = jnp.exp(-t_ref[0] * ad_ref[...])
    half = pl.program_id(0)
    sel = jnp.where(half == 0, h[:, :B_CH], -h[:, B_CH:])
    k_ref[...] = (sel * decay * m_ref[0]).astype(BF16)


def _hy_filter(zz, tt, mm, w1p, b1, w2, b2, w3, freq, absdelta, L):
    full = lambda shape: pl.BlockSpec(shape, lambda h: (0,) * len(shape))
    return pl.pallas_call(
        _hy_filter_kernel,
        out_shape=jax.ShapeDtypeStruct((L, 2 * B_CH), BF16),
        grid=(2,),
        in_specs=[pl.BlockSpec((1, L, LANES), lambda h: (h, 0, 0)),
                  pl.BlockSpec((1, L, 1), lambda h: (h, 0, 0)),
                  pl.BlockSpec((1, L, 1), lambda h: (h, 0, 0)),
                  full((LANES, HY_FFN)), full((1, HY_FFN)),
                  full((HY_INNER, HY_FFN, HY_FFN)), full((HY_INNER, 1, HY_FFN)),
                  full((HY_FFN, 2 * B_CH)), full((1, HY_FFN)), full((1, B_CH))],
        out_specs=pl.BlockSpec((L, B_CH), lambda h: (0, h)),
        compiler_params=_cparams(("parallel",)),
        name="hy_filter",
    )(zz, tt, mm, w1p, b1, w2, b2, w3, freq, absdelta)


def _dft_fwd_kernel(c_ref, s_ref, x_ref, oc_ref, os_ref):
    x = x_ref[...]
    oc_ref[...] = jnp.dot(c_ref[...], x, preferred_element_type=F32)
    os_ref[...] = jnp.dot(s_ref[...], x, preferred_element_type=F32)


def _dft_fwd(ctab, stab, x, L, tf=512):
    n = x.shape[1]
    out = jax.ShapeDtypeStruct((L, n), F32)
    return pl.pallas_call(
        _dft_fwd_kernel, out_shape=(out, out), grid=(L // tf,),
        in_specs=[pl.BlockSpec((tf, L), lambda i: (i, 0)),
                  pl.BlockSpec((tf, L), lambda i: (i, 0)),
                  pl.BlockSpec((L, n), lambda i: (0, 0))],
        out_specs=(pl.BlockSpec((tf, n), lambda i: (i, 0)),
                   pl.BlockSpec((tf, n), lambda i: (i, 0))),
        compiler_params=_cparams(("parallel",)), name="dft_filter",
    )(ctab, stab, x)


def _dft_prod_kernel(c_ref, s_ref, u_ref, hc_ref, hs_ref, yr_ref, yi_ref):
    u = u_ref[...]
    uc = jnp.dot(c_ref[...], u, preferred_element_type=F32)
    us = jnp.dot(s_ref[...], u, preferred_element_type=F32)
    tf = uc.shape[0]
    par = lax.broadcasted_iota(I32, (tf, B_CH), 0) % 2
    sgn = jnp.where(par == 0, 1.0, -1.0).astype(F32)
    hc, hs = hc_ref[...], hs_ref[...]
    kc = hc[:, :B_CH] - sgn * hs[:, B_CH:]
    ks = hs[:, :B_CH] + sgn * hc[:, B_CH:]
    yr_ref[...] = (uc * kc - us * ks).astype(BF16)
    yi_ref[...] = (uc * ks + us * kc).astype(BF16)


def _dft_prod(ctab, stab, u, hc, hs, B, L, tf=512):
    nf = L // tf
    out = jax.ShapeDtypeStruct((B * L, B_CH), BF16)
    return pl.pallas_call(
        _dft_prod_kernel, out_shape=(out, out), grid=(nf, B),
        in_specs=[pl.BlockSpec((tf, L), lambda f, b: (f, 0)),
                  pl.BlockSpec((tf, L), lambda f, b: (f, 0)),
                  pl.BlockSpec((L, B_CH), lambda f, b: (b, 0)),
                  pl.BlockSpec((tf, 2 * B_CH), lambda f, b: (f, 0)),
                  pl.BlockSpec((tf, 2 * B_CH), lambda f, b: (f, 0))],
        out_specs=(pl.BlockSpec((tf, B_CH), lambda f, b: (b * nf + f, 0)),
                   pl.BlockSpec((tf, B_CH), lambda f, b: (b * nf + f, 0))),
        compiler_params=_cparams(("parallel", "parallel")), name="dft_fwd_prod",
    )(ctab, stab, u, hc, hs)


def _dft_inv_kernel(ct_ref, st_ref, yr_ref, yi_ref, u_ref, x0_ref, bias_ref, o_ref, *, inv_l):
    y = (jnp.dot(ct_ref[...], yr_ref[...], preferred_element_type=F32)
         + jnp.dot(st_ref[...], yi_ref[...], preferred_element_type=F32)) * inv_l
    v = y + u_ref[...].astype(F32) * bias_ref[...]
    o_ref[...] = (v * x0_ref[...].astype(F32)).astype(BF16)


def _dft_inv(ctt, stt, yr, yi, u, x0, bias, B, L, tt=512):
    nt = L // tt
    return pl.pallas_call(
        functools.partial(_dft_inv_kernel, inv_l=1.0 / L),
        out_shape=jax.ShapeDtypeStruct((B * L, B_CH), BF16), grid=(nt, B),
        in_specs=[pl.BlockSpec((tt, L), lambda t, b: (t, 0)),
                  pl.BlockSpec((tt, L), lambda t, b: (t, 0)),
                  pl.BlockSpec((L, B_CH), lambda t, b: (b, 0)),
                  pl.BlockSpec((L, B_CH), lambda t, b: (b, 0)),
                  pl.BlockSpec((tt, B_CH), lambda t, b: (b * nt + t, 0)),
                  pl.BlockSpec((tt, B_CH), lambda t, b: (b * nt + t, 0)),
                  pl.BlockSpec((1, B_CH), lambda t, b: (0, 0))],
        out_specs=pl.BlockSpec((tt, B_CH), lambda t, b: (b * nt + t, 0)),
        compiler_params=_cparams(("parallel", "parallel")), name="dft_inv",
    )(ctt, stt, yr, yi, u, x0, bias)


def _hy_positions(L):
    t = jnp.linspace(0.0, 1.0, L, dtype=F32)[:, None]
    w = 2.0 * math.pi * jnp.arange(L, dtype=F32)[:, None] / L
    f = jnp.linspace(1e-4, HY_BANDS - 1, HY_BANDS, dtype=F32)[None, :]
    z = jnp.concatenate([t, jnp.cos(f * w), -jnp.sin(f * w)], axis=-1)

    def mirrored(a):
        return jnp.concatenate([a[:1], a[:0:-1]], axis=0)

    zz = jnp.pad(jnp.stack([z, mirrored(z)]), ((0, 0), (0, 0), (0, LANES - HY_EMB)))
    tt = jnp.stack([t, mirrored(t)])
    mm = jnp.ones((2, L, 1), F32).at[1, 0, 0].set(0.0)
    return zz, tt, mm


def _hyena(proj, p, tabs, B, L):
    ctab, stab, ctt, stt, zz, tt, mm = tabs
    u, x0 = _hy_prep(proj, p['hy_conv_w'], p['hy_conv_b'][None, :], B, L)
    max_decay = math.log(HY_TARGET) / HY_FAST
    min_decay = math.log(HY_TARGET) / HY_SLOW
    absdelta = jnp.abs(jnp.linspace(min_decay, max_decay, B_CH, dtype=F32))[None, :]
    w1p = jnp.pad(p['hy_w1'], ((0, LANES - HY_EMB), (0, 0)))
    k2 = _hy_filter(zz, tt, mm, w1p, p['hy_b1'][None, :], p['hy_w2'], p['hy_b2'][:, None, :],
                    p['hy_w3'], p['hy_freq'][None, :], absdelta, L)
    hc, hs = _dft_fwd(ctab, stab, k2, L)
    yr, yi = _dft_prod(ctab, stab, u, hc, hs, B, L)
    return _dft_inv(ctt, stt, yr, yi, u, x0, p['hy_bias'][None, :], B, L)


def _na_kernel(var_ref, q_ref, k_ref, v_ref, bias_ref, o_ref, *, rows):
    r = pl.program_id(1)
    wr = min(NA_WR, rows)
    kr0 = jnp.clip(r - wr // 2, 0, rows - wr)
    start = pl.multiple_of(kr0 * GRID_W, GRID_W)
    nk = wr * GRID_W
    q = q_ref[...]
    k = k_ref[pl.ds(start, nk), :]
    v = v_ref[pl.ds(start, nk), :]
    lane = lax.broadcasted_iota(I32, (GRID_W, LANES), 1)
    lane_o = lax.broadcasted_iota(I32, (GRID_W, GROUP_W), 1)
    qi = lax.broadcasted_iota(I32, (GRID_W, nk), 0)
    kc = lax.broadcasted_iota(I32, (GRID_W, nk), 1) % GRID_W
    c0 = jnp.clip(qi - NA_WC // 2, 0, GRID_W - NA_WC)
    ok = (kc >= c0) & (kc < c0 + NA_WC)
    acc = jnp.zeros((GRID_W, GROUP_W), F32)
    for h in range(C_HEADS):
        j, off = h // 2, (h % 2) * HEAD_DIM
        qc = q[:, j * LANES:(j + 1) * LANES]
        qm = jnp.where((lane >= off) & (lane < off + HEAD_DIM), qc, jnp.zeros_like(qc))
        s = lax.dot_general(qm, k[:, j * LANES:(j + 1) * LANES], (((1,), (1,)), ((), ())),
                            preferred_element_type=F32)
        s = jnp.where(ok, s + bias_ref[0, h], NEG)
        m = jnp.max(s, axis=-1, keepdims=True)
        e = jnp.exp2(s - m)
        l = jnp.sum(e, axis=-1, keepdims=True)
        pv = jnp.dot(e.astype(BF16), v, preferred_element_type=F32) * (1.0 / l)
        acc = jnp.where((lane_o >= h * HEAD_DIM) & (lane_o < (h + 1) * HEAD_DIM), pv, acc)
    o_ref[...] = acc.astype(BF16)


def _na_bias(rpb, rows):
    wr = min(NA_WR, rows)
    c = jnp.arange(GRID_W)
    dc = jnp.clip(c[None, :] - c[:, None], 1 - NA_WC, NA_WC - 1) + (NA_WC - 1)
    onehot = (dc[None, :, :] == jnp.arange(2 * NA_WC - 1)[:, None, None]).astype(F32)
    cols = jnp.einsum('hrd,dqk->hrqk', rpb.astype(F32) * LOG2E, onehot,
                      precision=lax.Precision.HIGHEST)
    variants = []
    for i in range(NA_WR):
        blk = cols[:, NA_WR - 1 - i:NA_WR - 1 - i + wr]
        variants.append(jnp.moveaxis(blk, 1, 2).reshape(C_HEADS, GRID_W, wr * GRID_W))
    return jnp.stack(variants)


def _na_attn(proj, bias, B, L):
    rows = L // GRID_W
    wr = min(NA_WR, rows)
    r = jnp.arange(rows)
    variant = (r - jnp.clip(r - wr // 2, 0, rows - wr)).astype(I32)
    nk = wr * GRID_W
    grid_spec = pltpu.PrefetchScalarGridSpec(
        num_scalar_prefetch=1, grid=(B, rows),
        in_specs=[pl.BlockSpec((GRID_W, GROUP_W), lambda b, r, var: (b * rows + r, COL_CQ)),
                  pl.BlockSpec((L, GROUP_W), lambda b, r, var: (b, COL_CK)),
                  pl.BlockSpec((L, GROUP_W), lambda b, r, var: (b, COL_CV)),
                  pl.BlockSpec((1, C_HEADS, GRID_W, nk), lambda b, r, var: (var[r], 0, 0, 0))],
        out_specs=pl.BlockSpec((GRID_W, GROUP_W), lambda b, r, var: (b * rows + r, 0)))
    return pl.pallas_call(
        functools.partial(_na_kernel, rows=rows),
        out_shape=jax.ShapeDtypeStruct((B * L, GROUP_W), BF16),
        grid_spec=grid_spec,
        compiler_params=_cparams(("parallel", "arbitrary")),
        name="na_attn",
    )(variant, proj, proj, proj, bias)


WQ_TILE = 256
WK_SPAN = WQ_TILE + 2 * WIN


def _win_kernel(sink_ref, q_ref, k_ref, v_ref, o_ref, *, L):
    i = pl.program_id(1)
    q0 = i * WQ_TILE
    k0 = pl.multiple_of(jnp.clip(q0 - WIN, 0, L - WK_SPAN), WIN)
    k = k_ref[pl.ds(k0, WK_SPAN), :]
    v = v_ref[pl.ds(k0, WK_SPAN), :]
    qf = q_ref[...].astype(F32)
    qpos = q0 + lax.broadcasted_iota(I32, (WQ_TILE, WK_SPAN), 0)
    kpos = k0 + lax.broadcasted_iota(I32, (WQ_TILE, WK_SPAN), 1)
    ok = jnp.abs(qpos - kpos) <= WIN
    lane = lax.broadcasted_iota(I32, (WQ_TILE, LANES), 1)
    for j in range(2):
        out = jnp.zeros((WQ_TILE, LANES), F32)
        for hh in range(2):
            h = 2 * j + hh
            g = h // (D_HEADS // D_KV_HEADS)
            qc = qf[:, j * LANES:(j + 1) * LANES]
            if hh != g:
                qc = pltpu.roll(qc, HEAD_DIM, 1)
            in_g = (lane >= g * HEAD_DIM) & (lane < (g + 1) * HEAD_DIM)
            qm = jnp.where(in_g, qc, 0.0).astype(BF16)
            s = lax.dot_general(qm, k, (((1,), (1,)), ((), ())), preferred_element_type=F32)
            s = jnp.where(ok, s, NEG)
            sk = sink_ref[h] * LOG2E
            m = jnp.maximum(jnp.max(s, axis=-1, keepdims=True), sk)
            e = jnp.exp2(s - m)
            l = jnp.sum(e, axis=-1, keepdims=True) + jnp.exp2(sk - m)
            pv = jnp.dot(e.astype(BF16), v, preferred_element_type=F32) * (1.0 / l)
            pv = jnp.where(in_g, pv, 0.0)
            if hh != g:
                pv = pltpu.roll(pv, HEAD_DIM, 1)
            out = out + pv
        o_ref[:, j * LANES:(j + 1) * LANES] = out.astype(BF16)


def _win_attn(proj, sink, B, L):
    nq = L // WQ_TILE
    return pl.pallas_call(
        functools.partial(_win_kernel, L=L),
        out_shape=jax.ShapeDtypeStruct((B * L, GROUP_W), BF16),
        grid=(B, nq),
        in_specs=[pl.BlockSpec(memory_space=pltpu.SMEM),
                  pl.BlockSpec((WQ_TILE, GROUP_W), lambda b, i: (b * nq + i, COL_DQ)),
                  pl.BlockSpec((L, LANES), lambda b, i: (b, COL_DK)),
                  pl.BlockSpec((L, LANES), lambda b, i: (b, COL_DV))],
        out_specs=pl.BlockSpec((WQ_TILE, GROUP_W), lambda b, i: (b * nq + i, 0)),
        compiler_params=_cparams(("parallel", "parallel")),
        name="win_attn",
    )(sink, proj, proj, proj)


def _layer_norm(y, g, b):
    mu = jnp.mean(y, axis=-1, keepdims=True)
    d = y - mu
    var = jnp.mean(d * d, axis=-1, keepdims=True)
    return d * lax.rsqrt(var + LN_EPS) * g + b


PACK_W = D_MODEL // 2
HI_MASK = -65536


def _pack_bf16_pairs(y):
    lo = lax.bitcast_convert_type(y[:, :PACK_W].astype(BF16).astype(F32), I32)
    hi = lax.bitcast_convert_type(y[:, PACK_W:].astype(BF16).astype(F32), I32)
    return lax.shift_right_logical(lo, 16) | (hi & HI_MASK)


def _unpack_bf16_pairs(w):
    return (lax.bitcast_convert_type(w << 16, F32), lax.bitcast_convert_type(w & HI_MASK, F32))


def _out_proj_kernel(oa_ref, ob_ref, oc_ref, od_ref, w_ref, x_ref, g_ref, b_ref, o_ref, p_ref):
    acc = DN_ALPHA * x_ref[...]
    for gi, ref in enumerate((oa_ref, ob_ref, oc_ref, od_ref)):
        acc = acc + jnp.dot(ref[...], w_ref[gi * GROUP_W:(gi + 1) * GROUP_W, :],
                            preferred_element_type=F32)
    y = _layer_norm(acc, g_ref[...], b_ref[...])
    o_ref[...] = y
    p_ref[...] = _pack_bf16_pairs(y)


def _out_proj(oa, ob, oc, od, w_bf, x, g, b, tm=512):
    T = x.shape[0]
    og = pl.BlockSpec((tm, GROUP_W), lambda i: (i, 0))
    row = pl.BlockSpec((tm, D_MODEL), lambda i: (i, 0))
    vec = pl.BlockSpec((1, D_MODEL), lambda i: (0, 0))
    return pl.pallas_call(
        _out_proj_kernel,
        out_shape=(jax.ShapeDtypeStruct((T, D_MODEL), F32), jax.ShapeDtypeStruct((T, PACK_W), I32)),
        grid=(T // tm,),
        in_specs=[og, og, og, og, pl.BlockSpec((D_MODEL, D_MODEL), lambda i: (0, 0)), row, vec, vec],
        out_specs=(row, pl.BlockSpec((tm, PACK_W), lambda i: (i, 0))),
        compiler_params=_cparams(("parallel",)),
        name="out_proj_ln",
    )(oa, ob, oc, od, w_bf, x, g, b)


RT_TM = 256


def _router_kernel(x_ref, wh_ref, wl_ref, bias_ref, tri_ref, te_ref, gate_ref, rank_ref, cnt_ref,
                   carry):
    @pl.when(pl.program_id(0) == 0)
    def _():
        carry[...] = jnp.zeros_like(carry)

    x = x_ref[...]
    xh = x.astype(BF16)
    xl = (x - xh.astype(F32)).astype(BF16)
    logits = (jnp.dot(xh, wh_ref[...], preferred_element_type=F32)
              + jnp.dot(xh, wl_ref[...], preferred_element_type=F32)
              + jnp.dot(xl, wh_ref[...], preferred_element_type=F32))
    scores = 1.0 / (1.0 + jnp.exp(-logits))
    sel = scores + bias_ref[...]
    lane_e = lax.broadcasted_iota(I32, (RT_TM, N_EXPERTS), 1).astype(F32)
    lane_k = lax.broadcasted_iota(I32, (RT_TM, LANES), 1)
    te = jnp.zeros((RT_TM, LANES), F32)
    ts = jnp.zeros((RT_TM, LANES), F32)
    onehot = jnp.zeros((RT_TM, N_EXPERTS), F32)
    for k in range(TOP_K):
        m = jnp.max(sel, axis=-1, keepdims=True)
        idx = jnp.min(jnp.where(sel == m, lane_e, float(N_EXPERTS)), axis=-1, keepdims=True)
        hit = lane_e == idx
        sc = jnp.sum(jnp.where(hit, scores, 0.0), axis=-1, keepdims=True)
        te = jnp.where(lane_k == k, idx, te)
        ts = jnp.where(lane_k == k, sc, ts)
        onehot = jnp.where(hit, 1.0, onehot)
        sel = jnp.where(hit, -jnp.inf, sel)
    gate_ref[...] = ts / jnp.sum(ts, axis=-1, keepdims=True) * ROUTED_SCALE
    te_ref[...] = te.astype(I32)
    before = jnp.dot(tri_ref[...], onehot.astype(BF16), preferred_element_type=F32) + carry[...]
    rank = jnp.zeros((RT_TM, LANES), F32)
    for k in range(TOP_K):
        rk = jnp.sum(jnp.where(lane_e == te[:, k:k + 1], before, 0.0), axis=-1, keepdims=True)
        rank = jnp.where(lane_k == k, rk, rank)
    rank_ref[...] = rank.astype(I32)
    carry[...] = carry[...] + jnp.sum(onehot, axis=0, keepdims=True)
    cnt_ref[...] = carry[...].astype(I32)


def _router(x1, wr, bias):
    T = x1.shape[0]
    wh = wr.astype(BF16)
    wl = (wr - wh.astype(F32)).astype(BF16)
    tri = (jnp.arange(RT_TM)[:, None] > jnp.arange(RT_TM)[None, :]).astype(BF16)
    row = pl.BlockSpec((RT_TM, LANES), lambda i: (i, 0))
    full = lambda shape: pl.BlockSpec(shape, lambda i: (0, 0))
    return pl.pallas_call(
        _router_kernel,
        out_shape=(jax.ShapeDtypeStruct((T, LANES), I32), jax.ShapeDtypeStruct((T, LANES), F32),
                   jax.ShapeDtypeStruct((T, LANES), I32), jax.ShapeDtypeStruct((1, N_EXPERTS), I32)),
        grid=(T // RT_TM,),
        in_specs=[pl.BlockSpec((RT_TM, D_MODEL), lambda i: (i, 0)),
                  full((D_MODEL, N_EXPERTS)), full((D_MODEL, N_EXPERTS)), full((1, N_EXPERTS)),
                  full((RT_TM, RT_TM))],
        out_specs=(row, row, row, full((1, N_EXPERTS))),
        scratch_shapes=[pltpu.VMEM((1, N_EXPERTS), F32)],
        compiler_params=_cparams(("arbitrary",)),
        name="router_topk",
    )(x1, wh, wl, bias[None, :], tri)


def _idx_copy(idx_hbm, idx_smem, isem, blk, slot):
    return pltpu.make_async_copy(idx_hbm.at[blk], idx_smem.at[slot], isem.at[slot])


ROW_SUB = D_MODEL // LANES


def _row_copy(src_hbm, buf, sem, idx_smem, slot, r):
    src = pl.multiple_of(idx_smem[slot, r] * ROW_SUB, ROW_SUB)
    return pltpu.make_async_copy(src_hbm.at[pl.ds(src, ROW_SUB), :],
                                 buf.at[slot, pl.ds(r * ROW_SUB, ROW_SUB), :], sem.at[slot])


def _tile_rows_chunk(buf_slot, row0, n_rows, c):
    return buf_slot[pl.ds(row0 * ROW_SUB + c, n_rows, stride=ROW_SUB), :]


def _gather_step(i, n, idx_hbm, src_hbm, idx_smem, buf, isem, sem, n_rows):
    def issue_rows(slot):
        for r in range(n_rows):
            _row_copy(src_hbm, buf, sem, idx_smem, slot, r).start()

    @pl.when(i == 0)
    def _():
        _idx_copy(idx_hbm, idx_smem, isem, 0, 0).start()
        _idx_copy(idx_hbm, idx_smem, isem, 0, 0).wait()
        issue_rows(0)

        @pl.when(n > 1)
        def _():
            _idx_copy(idx_hbm, idx_smem, isem, 1, 1).start()

    nxt = (i + 1) % 2
    cur = i % 2

    @pl.when(i + 1 < n)
    def _():
        _idx_copy(idx_hbm, idx_smem, isem, i + 1, nxt).wait()
        issue_rows(nxt)

    for r in range(n_rows):
        _row_copy(src_hbm, buf, sem, idx_smem, cur, r).wait()

    @pl.when(i + 2 < n)
    def _():
        _idx_copy(idx_hbm, idx_smem, isem, i + 2, cur).start()

    return cur


PACK_SUB = PACK_W // LANES
EXP_VMEM_LIMIT = 56 * 1024 * 1024


def _expert_kernel(blk_e_ref, nused_ref, tok_hbm, xp_hbm, wg_ref, wu_ref, wd_ref, y_ref,
                   idx_smem, xp_vmem, gbuf, wgb, wub, wdb, isem, xsem):
    i = pl.program_id(0)
    n = pl.num_programs(0)

    def idx_copy(blk, slot):
        return pltpu.make_async_copy(tok_hbm.at[blk], idx_smem.at[slot], isem.at[slot])

    @pl.when(i == 0)
    def _():
        load_all = pltpu.make_async_copy(xp_hbm, xp_vmem, xsem)
        load_all.start()
        idx_copy(0, 0).start()
        load_all.wait()

    cur = i % 2
    idx_copy(i, cur).wait()

    @pl.when(i + 1 < n)
    def _():
        idx_copy(i + 1, 1 - cur).start()

    prev_e = blk_e_ref[jnp.maximum(i - 1, 0)]

    @pl.when((i == 0) | (blk_e_ref[i] != prev_e))
    def _():
        wgb[...] = wg_ref[0, 0].astype(BF16)
        wub[...] = wu_ref[0, 0].astype(BF16)
        wdb[...] = wd_ref[0, 0].astype(BF16)

    @pl.when(i < nused_ref[0])
    def _():
        for r in range(EXP_BLK):
            gbuf[pl.ds(r * PACK_SUB, PACK_SUB), :] = xp_vmem[idx_smem[cur, r]]
        halves = [_unpack_bf16_pairs(gbuf[pl.ds(c, EXP_BLK, stride=PACK_SUB), :])
                  for c in range(PACK_SUB)]
        xb = jnp.concatenate([h[0] for h in halves] + [h[1] for h in halves], axis=1).astype(BF16)
        g = jnp.dot(xb, wgb[...], preferred_element_type=F32)
        u = jnp.dot(xb, wub[...], preferred_element_type=F32)
        h = (g * (1.0 / (1.0 + jnp.exp(-g))) * u).astype(BF16)
        y = jnp.dot(h, wdb[...], preferred_element_type=F32)
        for c in range(ROW_SUB):
            y_ref[pl.ds(c, EXP_BLK, stride=ROW_SUB), :] = y[:, c * LANES:(c + 1) * LANES]

    @pl.when(i >= nused_ref[0])
    def _():
        y_ref[...] = jnp.zeros_like(y_ref)


def _experts(xp, row_tok, blk_e, n_used, e_gate, e_up, e_down, layer):
    nblk = row_tok.shape[0]
    wmap = lambda i, be, nu: (layer, be[i], 0, 0)
    grid_spec = pltpu.PrefetchScalarGridSpec(
        num_scalar_prefetch=2, grid=(nblk,),
        in_specs=[pl.BlockSpec(memory_space=pl.ANY),
                  pl.BlockSpec(memory_space=pl.ANY),
                  pl.BlockSpec((1, 1, D_MODEL, D_EXPERT), wmap),
                  pl.BlockSpec((1, 1, D_MODEL, D_EXPERT), wmap),
                  pl.BlockSpec((1, 1, D_EXPERT, D_MODEL), wmap)],
        out_specs=pl.BlockSpec((EXP_BLK * ROW_SUB, LANES), lambda i, be, nu: (i, 0)),
        scratch_shapes=[pltpu.SMEM((2, EXP_BLK), I32),
                        pltpu.VMEM(xp.shape, I32),
                        pltpu.VMEM((PACK_SUB * EXP_BLK, LANES), I32),
                        pltpu.VMEM((D_MODEL, D_EXPERT), BF16),
                        pltpu.VMEM((D_MODEL, D_EXPERT), BF16),
                        pltpu.VMEM((D_EXPERT, D_MODEL), BF16),
                        pltpu.SemaphoreType.DMA((2,)),
                        pltpu.SemaphoreType.DMA])
    return pl.pallas_call(
        _expert_kernel,
        out_shape=jax.ShapeDtypeStruct((nblk * EXP_BLK * ROW_SUB, LANES), F32),
        grid_spec=grid_spec,
        compiler_params=pltpu.CompilerParams(dimension_semantics=("arbitrary",),
                                             vmem_limit_bytes=EXP_VMEM_LIMIT),
        name="moe_experts",
    )(blk_e, n_used, row_tok, xp, e_gate, e_up, e_down)


def _combine_kernel(dest_hbm, ys_hbm, x_ref, gate_ref, sg_ref, su_ref, sd_ref, g_ref, b_ref, o_ref,
                    idx_smem, buf, isem, sem):
    i = pl.program_id(0)
    n = pl.num_programs(0)
    slot = _gather_step(i, n, dest_hbm, ys_hbm, idx_smem, buf, isem, sem, CMB_TOK * TOP_K)
    x = x_ref[...]
    xb = x.astype(BF16)
    g = jnp.dot(xb, sg_ref[...], preferred_element_type=F32)
    u = jnp.dot(xb, su_ref[...], preferred_element_type=F32)
    h = (g * (1.0 / (1.0 + jnp.exp(-g))) * u).astype(BF16)
    acc = DN_ALPHA * x + jnp.dot(h, sd_ref[...], preferred_element_type=F32)
    gate = gate_ref[...]
    chunks = []
    for c in range(ROW_SUB):
        part = jnp.zeros((CMB_TOK, LANES), F32)
        for k in range(TOP_K):
            part = part + _tile_rows_chunk(buf.at[slot], k * CMB_TOK, CMB_TOK, c) * gate[:, k:k + 1]
        chunks.append(part)
    acc = acc + jnp.concatenate(chunks, axis=1)
    o_ref[...] = _layer_norm(acc, g_ref[...], b_ref[...])


def _combine(ys, dest_km, x1, gates, sg, su, sd, g, b):
    T = x1.shape[0]
    full = lambda shape: pl.BlockSpec(shape, lambda i: (0, 0))
    row = pl.BlockSpec((CMB_TOK, D_MODEL), lambda i: (i, 0))
    return pl.pallas_call(
        _combine_kernel,
        out_shape=jax.ShapeDtypeStruct((T, D_MODEL), F32),
        grid=(T // CMB_TOK,),
        in_specs=[pl.BlockSpec(memory_space=pl.ANY), pl.BlockSpec(memory_space=pl.ANY),
                  row, pl.BlockSpec((CMB_TOK, LANES), lambda i: (i, 0)),
                  full((D_MODEL, D_SHARED)), full((D_MODEL, D_SHARED)), full((D_SHARED, D_MODEL)),
                  full((1, D_MODEL)), full((1, D_MODEL))],
        out_specs=row,
        scratch_shapes=[pltpu.SMEM((2, CMB_TOK * TOP_K), I32),
                        pltpu.VMEM((2, CMB_TOK * TOP_K * ROW_SUB, LANES), F32),
                        pltpu.SemaphoreType.DMA((2,)),
                        pltpu.SemaphoreType.DMA((2,))],
        compiler_params=_cparams(("arbitrary",)),
        name="moe_combine_ln",
    )(dest_km, ys, x1, gates, sg, su, sd, g, b)


DEST_TM = 1024


def _dest_kernel(te_ref, rank_ref, ps_ref, d_ref):
    te = te_ref[...]
    lane_e = lax.broadcasted_iota(I32, (DEST_TM, N_EXPERTS), 1)
    lane_k = lax.broadcasted_iota(I32, (DEST_TM, LANES), 1)
    dest = jnp.zeros((DEST_TM, LANES), F32)
    for k in range(TOP_K):
        v = jnp.sum(jnp.where(lane_e == te[:, k:k + 1], ps_ref[...], 0.0), axis=-1, keepdims=True)
        dest = jnp.where(lane_k == k, v, dest)
    d_ref[...] = dest.astype(I32) + rank_ref[...]


def _dest_rows(te, rank, pstart):
    T = te.shape[0]
    row = pl.BlockSpec((DEST_TM, LANES), lambda i: (i, 0))
    return pl.pallas_call(
        _dest_kernel, out_shape=jax.ShapeDtypeStruct((T, LANES), I32), grid=(T // DEST_TM,),
        in_specs=[row, row, pl.BlockSpec((1, N_EXPERTS), lambda i: (0, 0))], out_specs=row,
        compiler_params=_cparams(("parallel",)), name="moe_dest",
    )(te, rank, pstart)


def _moe_ln(x1, xp, p, stacked, layer):
    T = x1.shape[0]
    te, gates, rank, counts = _router(x1, p['router_w'], p['router_bias'])
    counts = counts[0]
    padded = (counts + EXP_BLK - 1) // EXP_BLK * EXP_BLK
    pends = jnp.cumsum(padded)
    dest = _dest_rows(te, rank, (pends - padded).astype(F32)[None, :])[:, :TOP_K]
    nblk = (T * TOP_K + N_EXPERTS * (EXP_BLK - 1)) // EXP_BLK + 1
    tok = jnp.broadcast_to(jnp.arange(T, dtype=I32)[:, None], (T, TOP_K))
    row_tok = jnp.zeros((nblk * EXP_BLK,), I32).at[dest.reshape(-1)].set(
        tok.reshape(-1), unique_indices=True, mode='promise_in_bounds')
    blk_start = jnp.arange(nblk, dtype=I32) * EXP_BLK
    blk_e = jnp.minimum(jnp.sum((pends[None, :] <= blk_start[:, None]).astype(I32), axis=1),
                        N_EXPERTS - 1)
    n_used = (pends[-1] // EXP_BLK).astype(I32)[None]
    ys = _experts(xp.reshape(T, PACK_SUB, LANES), row_tok.reshape(nblk, EXP_BLK), blk_e, n_used,
                  stacked['e_gate'], stacked['e_up'], stacked['e_down'], layer)
    dest_km = dest.reshape(T // CMB_TOK, CMB_TOK, TOP_K).transpose(0, 2, 1).reshape(
        T // CMB_TOK, CMB_TOK * TOP_K)
    return _combine(ys, dest_km, x1, gates, p['s_gate'].astype(BF16), p['s_up'].astype(BF16),
                    p['s_down'].astype(BF16), p['ln2_g'][None, :], p['ln2_b'][None, :])


STACKED = ('e_gate', 'e_up', 'e_down')


def _encoder_layer(x, p, stacked, layer, lam_init, consts, B, L):
    ca, sa, cd, sd, gmat, hy_tabs = consts
    proj = _in_proj(x, p['w_in'].astype(BF16), ca, sa, cd, sd, L)
    lam = (jnp.exp(jnp.sum(p['lam_q1'] * p['lam_k1'])) - jnp.exp(jnp.sum(p['lam_q2'] * p['lam_k2']))
           + lam_init).astype(F32)[None]
    gn = (jnp.tile(p['a_subln_g'], A_HEADS) * (1.0 - lam_init))[None, :]
    o_a = _diff_attn(proj, lam, gn, gmat, B, L)
    o_b = _hyena(proj, p, hy_tabs, B, L)
    o_c = _na_attn(proj, _na_bias(p['na_rpb'], L // GRID_W), B, L)
    o_d = _win_attn(proj, p['d_sink'], B, L)
    x1, xp = _out_proj(o_a, o_b, o_c, o_d, p['w_out'].astype(BF16), x,
                       p['ln1_g'][None, :], p['ln1_b'][None, :])
    return _moe_ln(x1, xp, p, stacked, layer)


def _trunk(x3, params):
    B, L, _ = x3.shape
    ca, sa = _rope_tables(L, A_DH)
    cd, sd = _rope_tables(L, HEAD_DIM)
    head = jnp.arange(GROUP_W) // HEAD_DIM
    gmat = ((head[:, None] == head[None, :]).astype(F32) / HEAD_DIM).astype(BF16)
    hy_tabs = _trig_tables(L, False) + _trig_tables(L, True) + _hy_positions(L)
    consts = (ca, sa, cd, sd, gmat, hy_tabs)
    x = x3.reshape(B * L, D_MODEL)
    stacked = {name: params[name] for name in STACKED}
    for l in range(DEPTH):
        lp = {name: arr[l] for name, arr in params.items() if name not in STACKED}
        x = _encoder_layer(x, lp, stacked, l, 0.8 - 0.6 * math.exp(-0.3 * l), consts, B, L)
    return x.reshape(B, L, D_MODEL)


def kernel(x_prompt, x_sample, w_in, w_out, ln1_g, ln1_b, ln2_g, ln2_b, lam_q1, lam_k1, lam_q2,
           lam_k2, a_subln_g, hy_conv_w, hy_conv_b, hy_w1, hy_b1, hy_w2, hy_b2, hy_w3, hy_freq,
           hy_bias, na_rpb, d_sink, router_w, router_bias, e_gate, e_up, e_down, s_gate, s_up,
           s_down):
    params = {
        'w_in': w_in, 'w_out': w_out, 'ln1_g': ln1_g, 'ln1_b': ln1_b, 'ln2_g': ln2_g, 'ln2_b': ln2_b,
        'lam_q1': lam_q1, 'lam_k1': lam_k1, 'lam_q2': lam_q2, 'lam_k2': lam_k2,
        'a_subln_g': a_subln_g, 'hy_conv_w': hy_conv_w, 'hy_conv_b': hy_conv_b, 'hy_w1': hy_w1,
        'hy_b1': hy_b1, 'hy_w2': hy_w2, 'hy_b2': hy_b2, 'hy_w3': hy_w3, 'hy_freq': hy_freq,
        'hy_bias': hy_bias, 'na_rpb': na_rpb, 'd_sink': d_sink, 'router_w': router_w,
        'router_bias': router_bias, 'e_gate': e_gate, 'e_up': e_up, 'e_down': e_down,
        's_gate': s_gate, 's_up': s_up, 's_down': s_down,
    }
    return _trunk(x_prompt, params), _trunk(x_sample, params)
```

```python
import functools
import math

import jax
import jax.numpy as jnp
from jax import lax
from jax.experimental import pallas as pl
from jax.experimental.pallas import tpu as pltpu

F32 = jnp.float32
BF16 = jnp.bfloat16
I32 = jnp.int32

D_MODEL = 1024
DEPTH = 2
GRID_W = 64
HEAD_DIM = 64
GROUP_W = 256
A_HEADS = 4
A_DH = 32
B_CH = 256
HY_EMB = 33
HY_BANDS = 16
HY_FFN = 64
HY_INNER = 2
HY_TARGET = 1e-2
HY_FAST = 0.3
HY_SLOW = 1.5
C_HEADS = 4
NA_WR = 8
NA_WC = 16
D_HEADS = 4
D_KV_HEADS = 2
WIN = 128
D_IN = 2816
N_EXPERTS = 256
TOP_K = 8
D_EXPERT = 256
D_SHARED = 256
ROUTED_SCALE = 2.5
ROPE_THETA = 10000.0
LN_EPS = 1e-5
DN_ALPHA = (2 * DEPTH) ** 0.25
NEG = -1e30
LOG2E = 1.4426950408889634

LANES = 128
VMEM_LIMIT = 48 * 1024 * 1024

COL_AQ, COL_AK, COL_AV = 0, 1, 2
COL_B = 1
COL_CQ, COL_CK, COL_CV = 6, 7, 8
COL_DQ = 9
COL_DK, COL_DV = 20, 21

EXP_BLK = 256
CMB_TOK = 128


def _cparams(sem):
    return pltpu.CompilerParams(dimension_semantics=sem, vmem_limit_bytes=VMEM_LIMIT)


def _rope_lanes(y, cos, sin_signed, half):
    lane = lax.broadcasted_iota(I32, y.shape, 1)
    first = (lane % (2 * half)) < half
    rot = jnp.where(first, pltpu.roll(y, LANES - half, 1), pltpu.roll(y, half, 1))
    return y * cos + rot * sin_signed


def _in_proj_kernel(x_ref, w_ref, ca_ref, sa_ref, cd_ref, sd_ref, o_ref):
    xb = x_ref[...].astype(BF16)
    ca, sa, cd, sd = ca_ref[...], sa_ref[...], cd_ref[...], sd_ref[...]
    a_scale = (A_DH ** -0.5) * LOG2E
    hd_scale = (HEAD_DIM ** -0.5) * LOG2E
    for c in range(D_IN // 256):
        y = jnp.dot(xb, w_ref[:, c * 256:(c + 1) * 256], preferred_element_type=F32)
        for hh in range(2):
            ch = 2 * c + hh
            z = y[:, hh * LANES:(hh + 1) * LANES]
            if ch in (0, 1):
                z = _rope_lanes(z, ca, sa, A_DH // 2) * a_scale
            elif ch in (2, 3):
                z = _rope_lanes(z, ca, sa, A_DH // 2)
            elif ch in (12, 13):
                z = z * hd_scale
            elif ch in (18, 19):
                z = _rope_lanes(z, cd, sd, HEAD_DIM // 2) * hd_scale
            elif ch == 20:
                z = _rope_lanes(z, cd, sd, HEAD_DIM // 2)
            o_ref[:, ch * LANES:(ch + 1) * LANES] = z.astype(BF16)


def _in_proj(x, w_bf, ca, sa, cd, sd, L, tm=512):
    T = x.shape[0]
    nl = L // tm
    tab = pl.BlockSpec((tm, LANES), lambda i: (i % nl, 0))
    return pl.pallas_call(
        _in_proj_kernel,
        out_shape=jax.ShapeDtypeStruct((T, D_IN), BF16),
        grid=(T // tm,),
        in_specs=[pl.BlockSpec((tm, D_MODEL), lambda i: (i, 0)),
                  pl.BlockSpec((D_MODEL, D_IN), lambda i: (0, 0)),
                  tab, tab, tab, tab],
        out_specs=pl.BlockSpec((tm, D_IN), lambda i: (i, 0)),
        compiler_params=_cparams(("parallel",)),
        name="in_proj",
    )(x, w_bf, ca, sa, cd, sd)


def _rope_tables(L, dh):
    half = dh // 2
    lane = jnp.arange(LANES)
    inv_freq = ROPE_THETA ** (-(2.0 * (lane % half)).astype(F32) / dh)
    ang = jnp.arange(L, dtype=F32)[:, None] * inv_freq[None, :]
    sign = jnp.where((lane % dh) < half, -1.0, 1.0).astype(F32)
    return jnp.cos(ang), jnp.sin(ang) * sign[None, :]


def _diff_attn_kernel(lam_ref, q_ref, k_ref, v_ref, gn_ref, gm_ref, o_ref):
    lam = lam_ref[0]
    tq = q_ref.shape[0]
    q = q_ref[...]
    v = v_ref[...]
    lane = lax.broadcasted_iota(I32, (tq, LANES), 1)
    lane_o = lax.broadcasted_iota(I32, (tq, GROUP_W), 1)
    acc = jnp.zeros((tq, GROUP_W), F32)
    for h in range(A_HEADS):
        parts = []
        for c in range(2):
            comp = 2 * h + c
            j, off = comp // 4, (comp % 4) * A_DH
            qc = q[:, j * LANES:(j + 1) * LANES]
            qm = jnp.where((lane >= off) & (lane < off + A_DH), qc, jnp.zeros_like(qc))
            s = lax.dot_general(qm, k_ref[:, j * LANES:(j + 1) * LANES],
                                (((1,), (1,)), ((), ())), preferred_element_type=F32)
            m = jnp.max(s, axis=-1, keepdims=True)
            e = jnp.exp2(s - m)
            l = jnp.sum(e, axis=-1, keepdims=True)
            pv = jnp.dot(e.astype(BF16), v, preferred_element_type=F32)
            parts.append(pv * (1.0 / l))
        o_h = parts[0] - lam * parts[1]
        acc = jnp.where((lane_o >= h * HEAD_DIM) & (lane_o < (h + 1) * HEAD_DIM), o_h, acc)
    sq = acc * acc
    hi = sq.astype(BF16)
    lo = (sq - hi.astype(F32)).astype(BF16)
    ms = (jnp.dot(hi, gm_ref[...], preferred_element_type=F32)
          + jnp.dot(lo, gm_ref[...], preferred_element_type=F32))
    o_ref[...] = (acc * lax.rsqrt(ms + LN_EPS) * gn_ref[...]).astype(BF16)


def _diff_attn(proj, lam, gn, gmat, B, L, tq=256):
    nq = L // tq
    return pl.pallas_call(
        _diff_attn_kernel,
        out_shape=jax.ShapeDtypeStruct((B * L, GROUP_W), BF16),
        grid=(B, nq),
        in_specs=[pl.BlockSpec(memory_space=pltpu.SMEM),
                  pl.BlockSpec((tq, GROUP_W), lambda b, i: (b * nq + i, COL_AQ)),
                  pl.BlockSpec((L, GROUP_W), lambda b, i: (b, COL_AK)),
                  pl.BlockSpec((L, GROUP_W), lambda b, i: (b, COL_AV)),
                  pl.BlockSpec((1, GROUP_W), lambda b, i: (0, 0)),
                  pl.BlockSpec((GROUP_W, GROUP_W), lambda b, i: (0, 0))],
        out_specs=pl.BlockSpec((tq, GROUP_W), lambda b, i: (b * nq + i, 0)),
        compiler_params=_cparams(("parallel", "parallel")),
        name="diff_attn",
    )(lam, proj, proj, proj, gn, gmat)


HY_RC = 512


def _hy_prep_kernel(p_ref, w_ref, b_ref, u_ref, x0_ref):
    L = p_ref.shape[0]
    w0, w1, w2, bias = w_ref[0:1, :], w_ref[1:2, :], w_ref[2:3, :], b_ref[...]
    row = lax.broadcasted_iota(I32, (HY_RC, 3 * B_CH), 0)
    zero_row = jnp.zeros((1, 3 * B_CH), F32)
    for r in range(L // HY_RC):
        r0 = r * HY_RC
        xf = p_ref[r0:r0 + HY_RC, :].astype(F32)
        prev_row = zero_row if r == 0 else p_ref[r0 - 8:r0, :].astype(F32)[7:8, :]
        next_row = (zero_row if r0 + HY_RC == L
                    else p_ref[r0 + HY_RC:r0 + HY_RC + 8, :].astype(F32)[0:1, :])
        xp = jnp.where(row == 0, prev_row, pltpu.roll(xf, 1, 0))
        xn = jnp.where(row == HY_RC - 1, next_row, pltpu.roll(xf, HY_RC - 1, 0))
        y = bias + xp * w0 + xf * w1 + xn * w2
        x0, x1, v = y[:, :B_CH], y[:, B_CH:2 * B_CH], y[:, 2 * B_CH:]
        u_ref[r0:r0 + HY_RC, :] = (v * x1).astype(BF16)
        x0_ref[r0:r0 + HY_RC, :] = x0.astype(BF16)


def _hy_prep(proj, conv_w, conv_b, B, L):
    out = jax.ShapeDtypeStruct((B * L, B_CH), BF16)
    return pl.pallas_call(
        _hy_prep_kernel,
        out_shape=(out, out),
        grid=(B,),
        in_specs=[pl.BlockSpec((L, 3 * B_CH), lambda b: (b, COL_B)),
                  pl.BlockSpec((3, 3 * B_CH), lambda b: (0, 0)),
                  pl.BlockSpec((1, 3 * B_CH), lambda b: (0, 0))],
        out_specs=(pl.BlockSpec((L, B_CH), lambda b: (b, 0)),
                   pl.BlockSpec((L, B_CH), lambda b: (b, 0))),
        compiler_params=_cparams(("parallel",)),
        name="hy_prep",
    )(proj, conv_w, conv_b)


def _trig_table_kernel(ar_ref, ai_ref, br_ref, bi_ref, c_ref, s_ref):
    br, bi = br_ref[...], bi_ref[...]
    for th in range(c_ref.shape[1] // LANES):
        ar, ai = ar_ref[:, th:th + 1], ai_ref[:, th:th + 1]
        c_ref[:, th * LANES:(th + 1) * LANES] = (ar * br - ai * bi).astype(BF16)
        s_ref[:, th * LANES:(th + 1) * LANES] = (ar * bi + ai * br).astype(BF16)


def _trig_tables(L, transposed, tr=512):
    r = jnp.arange(L, dtype=I32)[:, None]
    hi = jnp.arange(L // LANES, dtype=I32)[None, :]
    lo = jnp.arange(LANES, dtype=I32)[None, :]
    if transposed:
        pa, pb = (2 * LANES * hi * r) % (4 * L), ((2 * lo + 1) * r) % (4 * L)
    else:
        pa, pb = ((2 * r + 1) * LANES * hi) % (4 * L), ((2 * r + 1) * lo) % (4 * L)
    ang_a = pa.astype(F32) * (math.pi / (2 * L))
    ang_b = pb.astype(F32) * (math.pi / (2 * L))
    out = jax.ShapeDtypeStruct((L, L), BF16)
    sa = pl.BlockSpec((tr, L // LANES), lambda i: (i, 0))
    sb = pl.BlockSpec((tr, LANES), lambda i: (i, 0))
    so = pl.BlockSpec((tr, L), lambda i: (i, 0))
    return pl.pallas_call(
        _trig_table_kernel, out_shape=(out, out), grid=(L // tr,),
        in_specs=[sa, sa, sb, sb], out_specs=(so, so),
        compiler_params=_cparams(("parallel",)), name="trig_tables",
    )(jnp.cos(ang_a), jnp.sin(ang_a), jnp.cos(ang_b), jnp.sin(ang_b))


def _hy_filter_kernel(z_ref, t_ref, m_ref, w1_ref, b1_ref, w2_ref, b2_ref, w3_ref, fr_ref,
                      ad_ref, k_ref):
    hp = lax.Precision.HIGHEST
    fr = fr_ref[...]
    h = jnp.sin(fr * (jnp.dot(z_ref[0], w1_ref[...], precision=hp, preferred_element_type=F32)
                      + b1_ref[...]))
    for i in range(HY_INNER):
        h = jnp.sin(fr * (jnp.dot(h, w2_ref[i], precision=hp, preferred_element_type=F32)
                          + b2_ref[i]))
    h = jnp.dot(h, w3_ref[...], precision=hp, preferred_element_type=F32)
    decay = jnp.exp(-t_ref[0] * ad_ref[...])
    half = pl.program_id(0)
    sel = jnp.where(half == 0, h[:, :B_CH], -h[:, B_CH:])
    k_ref[...] = (sel * decay * m_ref[0]).astype(BF16)


def _hy_filter(zz, tt, mm, w1p, b1, w2, b2, w3, freq, absdelta, L):
    full = lambda shape: pl.BlockSpec(shape, lambda h: (0,) * len(shape))
    return pl.pallas_call(
        _hy_filter_kernel,
        out_shape=jax.ShapeDtypeStruct((L, 2 * B_CH), BF16),
        grid=(2,),
        in_specs=[pl.BlockSpec((1, L, LANES), lambda h: (h, 0, 0)),
                  pl.BlockSpec((1, L, 1), lambda h: (h, 0, 0)),
                  pl.BlockSpec((1, L, 1), lambda h: (h, 0, 0)),
                  full((LANES, HY_FFN)), full((1, HY_FFN)),
                  full((HY_INNER, HY_FFN, HY_FFN)), full((HY_INNER, 1, HY_FFN)),
                  full((HY_FFN, 2 * B_CH)), full((1, HY_FFN)), full((1, B_CH))],
        out_specs=pl.BlockSpec((L, B_CH), lambda h: (0, h)),
        compiler_params=_cparams(("parallel",)),
        name="hy_filter",
    )(zz, tt, mm, w1p, b1, w2, b2, w3, freq, absdelta)


def _dft_fwd_kernel(c_ref, s_ref, x_ref, oc_ref, os_ref):
    x = x_ref[...]
    oc_ref[...] = jnp.dot(c_ref[...], x, preferred_element_type=F32)
    os_ref[...] = jnp.dot(s_ref[...], x, preferred_element_type=F32)


def _dft_fwd(ctab, stab, x, L, tf=512):
    n = x.shape[1]
    out = jax.ShapeDtypeStruct((L, n), F32)
    return pl.pallas_call(
        _dft_fwd_kernel, out_shape=(out, out), grid=(L // tf,),
        in_specs=[pl.BlockSpec((tf, L), lambda i: (i, 0)),
                  pl.BlockSpec((tf, L), lambda i: (i, 0)),
                  pl.BlockSpec((L, n), lambda i: (0, 0))],
        out_specs=(pl.BlockSpec((tf, n), lambda i: (i, 0)),
                   pl.BlockSpec((tf, n), lambda i: (i, 0))),
        compiler_params=_cparams(("parallel",)), name="dft_filter",
    )(ctab, stab, x)


def _dft_prod_kernel(c_ref, s_ref, u_ref, hc_ref, hs_ref, yr_ref, yi_ref):
    u = u_ref[...]
    uc = jnp.dot(c_ref[...], u, preferred_element_type=F32)
    us = jnp.dot(s_ref[...], u, preferred_element_type=F32)
    tf = uc.shape[0]
    par = lax.broadcasted_iota(I32, (tf, B_CH), 0) % 2
    sgn = jnp.where(par == 0, 1.0, -1.0).astype(F32)
    hc, hs = hc_ref[...], hs_ref[...]
    kc = hc[:, :B_CH] - sgn * hs[:, B_CH:]
    ks = hs[:, :B_CH] + sgn * hc[:, B_CH:]
    yr_ref[...] = (uc * kc - us * ks).astype(BF16)
    yi_ref[...] = (uc * ks + us * kc).astype(BF16)


def _dft_prod(ctab, stab, u, hc, hs, B, L, tf=512):
    nf = L // tf
    out = jax.ShapeDtypeStruct((B * L, B_CH), BF16)
    return pl.pallas_call(
        _dft_prod_kernel, out_shape=(out, out), grid=(nf, B),
        in_specs=[pl.BlockSpec((tf, L), lambda f, b: (f, 0)),
                  pl.BlockSpec((tf, L), lambda f, b: (f, 0)),
                  pl.BlockSpec((L, B_CH), lambda f, b: (b, 0)),
                  pl.BlockSpec((tf, 2 * B_CH), lambda f, b: (f, 0)),
                  pl.BlockSpec((tf, 2 * B_CH), lambda f, b: (f, 0))],
        out_specs=(pl.BlockSpec((tf, B_CH), lambda f, b: (b * nf + f, 0)),
                   pl.BlockSpec((tf, B_CH), lambda f, b: (b * nf + f, 0))),
        compiler_params=_cparams(("parallel", "parallel")), name="dft_fwd_prod",
    )(ctab, stab, u, hc, hs)


def _dft_inv_kernel(ct_ref, st_ref, yr_ref, yi_ref, u_ref, x0_ref, bias_ref, o_ref, *, inv_l):
    y = (jnp.dot(ct_ref[...], yr_ref[...], preferred_element_type=F32)
         + jnp.dot(st_ref[...], yi_ref[...], preferred_element_type=F32)) * inv_l
    v = y + u_ref[...].astype(F32) * bias_ref[...]
    o_ref[...] = (v * x0_ref[...].astype(F32)).astype(BF16)


def _dft_inv(ctt, stt, yr, yi, u, x0, bias, B, L, tt=512):
    nt = L // tt
    return pl.pallas_call(
        functools.partial(_dft_inv_kernel, inv_l=1.0 / L),
        out_shape=jax.ShapeDtypeStruct((B * L, B_CH), BF16), grid=(nt, B),
        in_specs=[pl.BlockSpec((tt, L), lambda t, b: (t, 0)),
                  pl.BlockSpec((tt, L), lambda t, b: (t, 0)),
                  pl.BlockSpec((L, B_CH), lambda t, b: (b, 0)),
                  pl.BlockSpec((L, B_CH), lambda t, b: (b, 0)),
                  pl.BlockSpec((tt, B_CH), lambda t, b: (b * nt + t, 0)),
                  pl.BlockSpec((tt, B_CH), lambda t, b: (b * nt + t, 0)),
                  pl.BlockSpec((1, B_CH), lambda t, b: (0, 0))],
        out_specs=pl.BlockSpec((tt, B_CH), lambda t, b: (b * nt + t, 0)),
        compiler_params=_cparams(("parallel", "parallel")), name="dft_inv",
    )(ctt, stt, yr, yi, u, x0, bias)


def _hy_positions(L):
    t = jnp.linspace(0.0, 1.0, L, dtype=F32)[:, None]
    w = 2.0 * math.pi * jnp.arange(L, dtype=F32)[:, None] / L
    f = jnp.linspace(1e-4, HY_BANDS - 1, HY_BANDS, dtype=F32)[None, :]
    z = jnp.concatenate([t, jnp.cos(f * w), -jnp.sin(f * w)], axis=-1)

    def mirrored(a):
        return jnp.concatenate([a[:1], a[:0:-1]], axis=0)

    zz = jnp.pad(jnp.stack([z, mirrored(z)]), ((0, 0), (0, 0), (0, LANES - HY_EMB)))
    tt = jnp.stack([t, mirrored(t)])
    mm = jnp.ones((2, L, 1), F32).at[1, 0, 0].set(0.0)
    return zz, tt, mm


def _hyena(proj, p, tabs, B, L):
    ctab, stab, ctt, stt, zz, tt, mm = tabs
    u, x0 = _hy_prep(proj, p['hy_conv_w'], p['hy_conv_b'][None, :], B, L)
    max_decay = math.log(HY_TARGET) / HY_FAST
    min_decay = math.log(HY_TARGET) / HY_SLOW
    absdelta = jnp.abs(jnp.linspace(min_decay, max_decay, B_CH, dtype=F32))[None, :]
    w1p = jnp.pad(p['hy_w1'], ((0, LANES - HY_EMB), (0, 0)))
    k2 = _hy_filter(zz, tt, mm, w1p, p['hy_b1'][None, :], p['hy_w2'], p['hy_b2'][:, None, :],
                    p['hy_w3'], p['hy_freq'][None, :], absdelta, L)
    hc, hs = _dft_fwd(ctab, stab, k2, L)
    yr, yi = _dft_prod(ctab, stab, u, hc, hs, B, L)
    return _dft_inv(ctt, stt, yr, yi, u, x0, p['hy_bias'][None, :], B, L)


def _na_kernel(var_ref, q_ref, k_ref, v_ref, bias_ref, o_ref, *, rows):
    r = pl.program_id(1)
    wr = min(NA_WR, rows)
    kr0 = jnp.clip(r - wr // 2, 0, rows - wr)
    start = pl.multiple_of(kr0 * GRID_W, GRID_W)
    nk = wr * GRID_W
    q = q_ref[...]
    k = k_ref[pl.ds(start, nk), :]
    v = v_ref[pl.ds(start, nk), :]
    lane = lax.broadcasted_iota(I32, (GRID_W, LANES), 1)
    lane_o = lax.broadcasted_iota(I32, (GRID_W, GROUP_W), 1)
    qi = lax.broadcasted_iota(I32, (GRID_W, nk), 0)
    kc = lax.broadcasted_iota(I32, (GRID_W, nk), 1) % GRID_W
    c0 = jnp.clip(qi - NA_WC // 2, 0, GRID_W - NA_WC)
    ok = (kc >= c0) & (kc < c0 + NA_WC)
    acc = jnp.zeros((GRID_W, GROUP_W), F32)
    for h in range(C_HEADS):
        j, off = h // 2, (h % 2) * HEAD_DIM
        qc = q[:, j * LANES:(j + 1) * LANES]
        qm = jnp.where((lane >= off) & (lane < off + HEAD_DIM), qc, jnp.zeros_like(qc))
        s = lax.dot_general(qm, k[:, j * LANES:(j + 1) * LANES], (((1,), (1,)), ((), ())),
                            preferred_element_type=F32)
        s = jnp.where(ok, s + bias_ref[0, h], NEG)
        m = jnp.max(s, axis=-1, keepdims=True)
        e = jnp.exp2(s - m)
        l = jnp.sum(e, axis=-1, keepdims=True)
        pv = jnp.dot(e.astype(BF16), v, preferred_element_type=F32) * (1.0 / l)
        acc = jnp.where((lane_o >= h * HEAD_DIM) & (lane_o < (h + 1) * HEAD_DIM), pv, acc)
    o_ref[...] = acc.astype(BF16)


def _na_bias(rpb, rows):
    wr = min(NA_WR, rows)
    c = jnp.arange(GRID_W)
    dc = jnp.clip(c[None, :] - c[:, None], 1 - NA_WC, NA_WC - 1) + (NA_WC - 1)
    onehot = (dc[None, :, :] == jnp.arange(2 * NA_WC - 1)[:, None, None]).astype(F32)
    cols = jnp.einsum('hrd,dqk->hrqk', rpb.astype(F32) * LOG2E, onehot,
                      precision=lax.Precision.HIGHEST)
    variants = []
    for i in range(NA_WR):
        blk = cols[:, NA_WR - 1 - i:NA_WR - 1 - i + wr]
        variants.append(jnp.moveaxis(blk, 1, 2).reshape(C_HEADS, GRID_W, wr * GRID_W))
    return jnp.stack(variants)


def _na_attn(proj, bias, B, L):
    rows = L // GRID_W
    wr = min(NA_WR, rows)
    r = jnp.arange(rows)
    variant = (r - jnp.clip(r - wr // 2, 0, rows - wr)).astype(I32)
    nk = wr * GRID_W
    grid_spec = pltpu.PrefetchScalarGridSpec(
        num_scalar_prefetch=1, grid=(B, rows),
        in_specs=[pl.BlockSpec((GRID_W, GROUP_W), lambda b, r, var: (b * rows + r, COL_CQ)),
                  pl.BlockSpec((L, GROUP_W), lambda b, r, var: (b, COL_CK)),
                  pl.BlockSpec((L, GROUP_W), lambda b, r, var: (b, COL_CV)),
                  pl.BlockSpec((1, C_HEADS, GRID_W, nk), lambda b, r, var: (var[r], 0, 0, 0))],
        out_specs=pl.BlockSpec((GRID_W, GROUP_W), lambda b, r, var: (b * rows + r, 0)))
    return pl.pallas_call(
        functools.partial(_na_kernel, rows=rows),
        out_shape=jax.ShapeDtypeStruct((B * L, GROUP_W), BF16),
        grid_spec=grid_spec,
        compiler_params=_cparams(("parallel", "arbitrary")),
        name="na_attn",
    )(variant, proj, proj, proj, bias)


WQ_TILE = 256
WK_SPAN = WQ_TILE + 2 * WIN


def _win_kernel(sink_ref, q_ref, k_ref, v_ref, o_ref, *, L):
    i = pl.program_id(1)
    q0 = i * WQ_TILE
    k0 = pl.multiple_of(jnp.clip(q0 - WIN, 0, L - WK_SPAN), WIN)
    k = k_ref[pl.ds(k0, WK_SPAN), :]
    v = v_ref[pl.ds(k0, WK_SPAN), :]
    qf = q_ref[...].astype(F32)
    qpos = q0 + lax.broadcasted_iota(I32, (WQ_TILE, WK_SPAN), 0)
    kpos = k0 + lax.broadcasted_iota(I32, (WQ_TILE, WK_SPAN), 1)
    ok = jnp.abs(qpos - kpos) <= WIN
    lane = lax.broadcasted_iota(I32, (WQ_TILE, LANES), 1)
    for j in range(2):
        out = jnp.zeros((WQ_TILE, LANES), F32)
        for hh in range(2):
            h = 2 * j + hh
            g = h // (D_HEADS // D_KV_HEADS)
            qc = qf[:, j * LANES:(j + 1) * LANES]
            if hh != g:
                qc = pltpu.roll(qc, HEAD_DIM, 1)
            in_g = (lane >= g * HEAD_DIM) & (lane < (g + 1) * HEAD_DIM)
            qm = jnp.where(in_g, qc, 0.0).astype(BF16)
            s = lax.dot_general(qm, k, (((1,), (1,)), ((), ())), preferred_element_type=F32)
            s = jnp.where(ok, s, NEG)
            sk = sink_ref[h] * LOG2E
            m = jnp.maximum(jnp.max(s, axis=-1, keepdims=True), sk)
            e = jnp.exp2(s - m)
            l = jnp.sum(e, axis=-1, keepdims=True) + jnp.exp2(sk - m)
            pv = jnp.dot(e.astype(BF16), v, preferred_element_type=F32) * (1.0 / l)
            pv = jnp.where(in_g, pv, 0.0)
            if hh != g:
                pv = pltpu.roll(pv, HEAD_DIM, 1)
            out = out + pv
        o_ref[:, j * LANES:(j + 1) * LANES] = out.astype(BF16)


def _win_attn(proj, sink, B, L):
    nq = L // WQ_TILE
    return pl.pallas_call(
        functools.partial(_win_kernel, L=L),
        out_shape=jax.ShapeDtypeStruct((B * L, GROUP_W), BF16),
        grid=(B, nq),
        in_specs=[pl.BlockSpec(memory_space=pltpu.SMEM),
                  pl.BlockSpec((WQ_TILE, GROUP_W), lambda b, i: (b * nq + i, COL_DQ)),
                  pl.BlockSpec((L, LANES), lambda b, i: (b, COL_DK)),
                  pl.BlockSpec((L, LANES), lambda b, i: (b, COL_DV))],
        out_specs=pl.BlockSpec((WQ_TILE, GROUP_W), lambda b, i: (b * nq + i, 0)),
        compiler_params=_cparams(("parallel", "parallel")),
        name="win_attn",
    )(sink, proj, proj, proj)


def _layer_norm(y, g, b):
    mu = jnp.mean(y, axis=-1, keepdims=True)
    d = y - mu
    var = jnp.mean(d * d, axis=-1, keepdims=True)
    return d * lax.rsqrt(var + LN_EPS) * g + b


PACK_W = D_MODEL // 2
HI_MASK = -65536


def _pack_bf16_pairs(y):
    lo = lax.bitcast_convert_type(y[:, :PACK_W].astype(BF16).astype(F32), I32)
    hi = lax.bitcast_convert_type(y[:, PACK_W:].astype(BF16).astype(F32), I32)
    return lax.shift_right_logical(lo, 16) | (hi & HI_MASK)


def _unpack_bf16_pairs(w):
    return (lax.bitcast_convert_type(w << 16, F32), lax.bitcast_convert_type(w & HI_MASK, F32))


def _out_proj_kernel(oa_ref, ob_ref, oc_ref, od_ref, w_ref, x_ref, g_ref, b_ref, o_ref, p_ref):
    acc = DN_ALPHA * x_ref[...]
    for gi, ref in enumerate((oa_ref, ob_ref, oc_ref, od_ref)):
        acc = acc + jnp.dot(ref[...], w_ref[gi * GROUP_W:(gi + 1) * GROUP_W, :],
                            preferred_element_type=F32)
    y = _layer_norm(acc, g_ref[...], b_ref[...])
    o_ref[...] = y
    p_ref[...] = _pack_bf16_pairs(y)


def _out_proj(oa, ob, oc, od, w_bf, x, g, b, tm=512):
    T = x.shape[0]
    og = pl.BlockSpec((tm, GROUP_W), lambda i: (i, 0))
    row = pl.BlockSpec((tm, D_MODEL), lambda i: (i, 0))
    vec = pl.BlockSpec((1, D_MODEL), lambda i: (0, 0))
    return pl.pallas_call(
        _out_proj_kernel,
        out_shape=(jax.ShapeDtypeStruct((T, D_MODEL), F32), jax.ShapeDtypeStruct((T, PACK_W), I32)),
        grid=(T // tm,),
        in_specs=[og, og, og, og, pl.BlockSpec((D_MODEL, D_MODEL), lambda i: (0, 0)), row, vec, vec],
        out_specs=(row, pl.BlockSpec((tm, PACK_W), lambda i: (i, 0))),
        compiler_params=_cparams(("parallel",)),
        name="out_proj_ln",
    )(oa, ob, oc, od, w_bf, x, g, b)


RT_TM = 256


def _router_kernel(x_ref, wh_ref, wl_ref, bias_ref, tri_ref, te_ref, gate_ref, rank_ref, cnt_ref,
                   carry):
    @pl.when(pl.program_id(0) == 0)
    def _():
        carry[...] = jnp.zeros_like(carry)

    x = x_ref[...]
    xh = x.astype(BF16)
    xl = (x - xh.astype(F32)).astype(BF16)
    logits = (jnp.dot(xh, wh_ref[...], preferred_element_type=F32)
              + jnp.dot(xh, wl_ref[...], preferred_element_type=F32)
              + jnp.dot(xl, wh_ref[...], preferred_element_type=F32))
    scores = 1.0 / (1.0 + jnp.exp(-logits))
    sel = scores + bias_ref[...]
    lane_e = lax.broadcasted_iota(I32, (RT_TM, N_EXPERTS), 1).astype(F32)
    lane_k = lax.broadcasted_iota(I32, (RT_TM, LANES), 1)
    te = jnp.zeros((RT_TM, LANES), F32)
    ts = jnp.zeros((RT_TM, LANES), F32)
    onehot = jnp.zeros((RT_TM, N_EXPERTS), F32)
    for k in range(TOP_K):
        m = jnp.max(sel, axis=-1, keepdims=True)
        idx = jnp.min(jnp.where(sel == m, lane_e, float(N_EXPERTS)), axis=-1, keepdims=True)
        hit = lane_e == idx
        sc = jnp.sum(jnp.where(hit, scores, 0.0), axis=-1, keepdims=True)
        te = jnp.where(lane_k == k, idx, te)
        ts = jnp.where(lane_k == k, sc, ts)
        onehot = jnp.where(hit, 1.0, onehot)
        sel = jnp.where(hit, -jnp.inf, sel)
    gate_ref[...] = ts / jnp.sum(ts, axis=-1, keepdims=True) * ROUTED_SCALE
    te_ref[...] = te.astype(I32)
    before = jnp.dot(tri_ref[...], onehot.astype(BF16), preferred_element_type=F32) + carry[...]
    rank = jnp.zeros((RT_TM, LANES), F32)
    for k in range(TOP_K):
        rk = jnp.sum(jnp.where(lane_e == te[:, k:k + 1], before, 0.0), axis=-1, keepdims=True)
        rank = jnp.where(lane_k == k, rk, rank)
    rank_ref[...] = rank.astype(I32)
    carry[...] = carry[...] + jnp.sum(onehot, axis=0, keepdims=True)
    cnt_ref[...] = carry[...].astype(I32)


def _router(x1, wr, bias):
    T = x1.shape[0]
    wh = wr.astype(BF16)
    wl = (wr - wh.astype(F32)).astype(BF16)
    tri = (jnp.arange(RT_TM)[:, None] > jnp.arange(RT_TM)[None, :]).astype(BF16)
    row = pl.BlockSpec((RT_TM, LANES), lambda i: (i, 0))
    full = lambda shape: pl.BlockSpec(shape, lambda i: (0, 0))
    return pl.pallas_call(
        _router_kernel,
        out_shape=(jax.ShapeDtypeStruct((T, LANES), I32), jax.ShapeDtypeStruct((T, LANES), F32),
                   jax.ShapeDtypeStruct((T, LANES), I32), jax.ShapeDtypeStruct((1, N_EXPERTS), I32)),
        grid=(T // RT_TM,),
        in_specs=[pl.BlockSpec((RT_TM, D_MODEL), lambda i: (i, 0)),
                  full((D_MODEL, N_EXPERTS)), full((D_MODEL, N_EXPERTS)), full((1, N_EXPERTS)),
                  full((RT_TM, RT_TM))],
        out_specs=(row, row, row, full((1, N_EXPERTS))),
        scratch_shapes=[pltpu.VMEM((1, N_EXPERTS), F32)],
        compiler_params=_cparams(("arbitrary",)),
        name="router_topk",
    )(x1, wh, wl, bias[None, :], tri)


def _idx_copy(idx_hbm, idx_smem, isem, blk, slot):
    return pltpu.make_async_copy(idx_hbm.at[blk], idx_smem.at[slot], isem.at[slot])


ROW_SUB = D_MODEL // LANES


IDX_SLOTS = 3


def _row_copy(src_hbm, buf, sem, idx_smem, islot, bslot, r):
    src = pl.multiple_of(idx_smem[islot, r] * ROW_SUB, ROW_SUB)
    return pltpu.make_async_copy(src_hbm.at[pl.ds(src, ROW_SUB), :],
                                 buf.at[bslot, pl.ds(r * ROW_SUB, ROW_SUB), :], sem.at[bslot])


def _tile_rows_chunk(buf_slot, row0, n_rows, c):
    return buf_slot[pl.ds(row0 * ROW_SUB + c, n_rows, stride=ROW_SUB), :]


def _gather_step(i, n, idx_hbm, src_hbm, idx_smem, buf, isem, sem, n_rows):
    def issue_rows(islot, bslot):
        for r in range(n_rows):
            _row_copy(src_hbm, buf, sem, idx_smem, islot, bslot, r).start()

    @pl.when(i == 0)
    def _():
        _idx_copy(idx_hbm, idx_smem, isem, 0, 0).start()
        _idx_copy(idx_hbm, idx_smem, isem, 0, 0).wait()
        issue_rows(0, 0)

        @pl.when(n > 1)
        def _():
            _idx_copy(idx_hbm, idx_smem, isem, 1, 1).start()

    @pl.when(i + 2 < n)
    def _():
        _idx_copy(idx_hbm, idx_smem, isem, i + 2, (i + 2) % IDX_SLOTS).start()

    cur = i % 2

    @pl.when(i + 1 < n)
    def _():
        _idx_copy(idx_hbm, idx_smem, isem, i + 1, (i + 1) % IDX_SLOTS).wait()
        issue_rows((i + 1) % IDX_SLOTS, 1 - cur)

    for r in range(n_rows):
        _row_copy(src_hbm, buf, sem, idx_smem, i % IDX_SLOTS, cur, r).wait()

    return cur


PACK_SUB = PACK_W // LANES
EXP_VMEM_LIMIT = 56 * 1024 * 1024


def _expert_kernel(blk_e_ref, nused_ref, tok_hbm, xp_hbm, wg_ref, wu_ref, wd_ref, y_ref,
                   idx_smem, xp_vmem, gbuf, wgb, wub, wdb, isem, xsem):
    i = pl.program_id(0)
    n = pl.num_programs(0)

    def idx_copy(blk, slot):
        return pltpu.make_async_copy(tok_hbm.at[blk], idx_smem.at[slot], isem.at[slot])

    @pl.when(i == 0)
    def _():
        load_all = pltpu.make_async_copy(xp_hbm, xp_vmem, xsem)
        load_all.start()
        idx_copy(0, 0).start()
        load_all.wait()

    cur = i % 2
    idx_copy(i, cur).wait()

    @pl.when(i + 1 < n)
    def _():
        idx_copy(i + 1, 1 - cur).start()

    prev_e = blk_e_ref[jnp.maximum(i - 1, 0)]

    @pl.when((i == 0) | (blk_e_ref[i] != prev_e))
    def _():
        wgb[...] = wg_ref[0, 0].astype(BF16)
        wub[...] = wu_ref[0, 0].astype(BF16)
        wdb[...] = wd_ref[0, 0].astype(BF16)

    @pl.when(i < nused_ref[0])
    def _():
        for r in range(EXP_BLK):
            gbuf[pl.ds(r * PACK_SUB, PACK_SUB), :] = xp_vmem[idx_smem[cur, r]]
        halves = [_unpack_bf16_pairs(gbuf[pl.ds(c, EXP_BLK, stride=PACK_SUB), :])
                  for c in range(PACK_SUB)]
        xb = jnp.concatenate([h[0] for h in halves] + [h[1] for h in halves], axis=1).astype(BF16)
        g = jnp.dot(xb, wgb[...], preferred_element_type=F32)
        u = jnp.dot(xb, wub[...], preferred_element_type=F32)
        h = (g * (1.0 / (1.0 + jnp.exp(-g))) * u).astype(BF16)
        y = jnp.dot(h, wdb[...], preferred_element_type=F32)
        for c in range(ROW_SUB):
            y_ref[pl.ds(c, EXP_BLK, stride=ROW_SUB), :] = y[:, c * LANES:(c + 1) * LANES]

    @pl.when(i >= nused_ref[0])
    def _():
        y_ref[...] = jnp.zeros_like(y_ref)


def _experts(xp, row_tok, blk_e, n_used, e_gate, e_up, e_down, layer):
    nblk = row_tok.shape[0]
    wmap = lambda i, be, nu: (layer, be[i], 0, 0)
    grid_spec = pltpu.PrefetchScalarGridSpec(
        num_scalar_prefetch=2, grid=(nblk,),
        in_specs=[pl.BlockSpec(memory_space=pl.ANY),
                  pl.BlockSpec(memory_space=pl.ANY),
                  pl.BlockSpec((1, 1, D_MODEL, D_EXPERT), wmap),
                  pl.BlockSpec((1, 1, D_MODEL, D_EXPERT), wmap),
                  pl.BlockSpec((1, 1, D_EXPERT, D_MODEL), wmap)],
        out_specs=pl.BlockSpec((EXP_BLK * ROW_SUB, LANES), lambda i, be, nu: (i, 0)),
        scratch_shapes=[pltpu.SMEM((2, EXP_BLK), I32),
                        pltpu.VMEM(xp.shape, I32),
                        pltpu.VMEM((PACK_SUB * EXP_BLK, LANES), I32),
                        pltpu.VMEM((D_MODEL, D_EXPERT), BF16),
                        pltpu.VMEM((D_MODEL, D_EXPERT), BF16),
                        pltpu.VMEM((D_EXPERT, D_MODEL), BF16),
                        pltpu.SemaphoreType.DMA((2,)),
                        pltpu.SemaphoreType.DMA])
    return pl.pallas_call(
        _expert_kernel,
        out_shape=jax.ShapeDtypeStruct((nblk * EXP_BLK * ROW_SUB, LANES), F32),
        grid_spec=grid_spec,
        compiler_params=pltpu.CompilerParams(dimension_semantics=("arbitrary",),
                                             vmem_limit_bytes=EXP_VMEM_LIMIT),
        name="moe_experts",
    )(blk_e, n_used, row_tok, xp, e_gate, e_up, e_down)


def _combine_kernel(dest_hbm, ys_hbm, x_ref, gate_ref, sg_ref, su_ref, sd_ref, g_ref, b_ref, o_ref,
                    idx_smem, buf, isem, sem):
    i = pl.program_id(0)
    n = pl.num_programs(0)
    slot = _gather_step(i, n, dest_hbm, ys_hbm, idx_smem, buf, isem, sem, CMB_TOK * TOP_K)
    x = x_ref[...]
    xb = x.astype(BF16)
    g = jnp.dot(xb, sg_ref[...], preferred_element_type=F32)
    u = jnp.dot(xb, su_ref[...], preferred_element_type=F32)
    h = (g * (1.0 / (1.0 + jnp.exp(-g))) * u).astype(BF16)
    acc = DN_ALPHA * x + jnp.dot(h, sd_ref[...], preferred_element_type=F32)
    gate = gate_ref[...]
    chunks = []
    for c in range(ROW_SUB):
        part = jnp.zeros((CMB_TOK, LANES), F32)
        for k in range(TOP_K):
            part = part + _tile_rows_chunk(buf.at[slot], k * CMB_TOK, CMB_TOK, c) * gate[:, k:k + 1]
        chunks.append(part)
    acc = acc + jnp.concatenate(chunks, axis=1)
    o_ref[...] = _layer_norm(acc, g_ref[...], b_ref[...])


def _combine(ys, dest_km, x1, gates, sg, su, sd, g, b):
    T = x1.shape[0]
    full = lambda shape: pl.BlockSpec(shape, lambda i: (0, 0))
    row = pl.BlockSpec((CMB_TOK, D_MODEL), lambda i: (i, 0))
    return pl.pallas_call(
        _combine_kernel,
        out_shape=jax.ShapeDtypeStruct((T, D_MODEL), F32),
        grid=(T // CMB_TOK,),
        in_specs=[pl.BlockSpec(memory_space=pl.ANY), pl.BlockSpec(memory_space=pl.ANY),
                  row, pl.BlockSpec((CMB_TOK, LANES), lambda i: (i, 0)),
                  full((D_MODEL, D_SHARED)), full((D_MODEL, D_SHARED)), full((D_SHARED, D_MODEL)),
                  full((1, D_MODEL)), full((1, D_MODEL))],
        out_specs=row,
        scratch_shapes=[pltpu.SMEM((IDX_SLOTS, CMB_TOK * TOP_K), I32),
                        pltpu.VMEM((2, CMB_TOK * TOP_K * ROW_SUB, LANES), F32),
                        pltpu.SemaphoreType.DMA((IDX_SLOTS,)),
                        pltpu.SemaphoreType.DMA((2,))],
        compiler_params=_cparams(("arbitrary",)),
        name="moe_combine_ln",
    )(dest_km, ys, x1, gates, sg, su, sd, g, b)


DEST_TM = 1024


def _dest_kernel(te_ref, rank_ref, ps_ref, d_ref):
    te = te_ref[...]
    lane_e = lax.broadcasted_iota(I32, (DEST_TM, N_EXPERTS), 1)
    lane_k = lax.broadcasted_iota(I32, (DEST_TM, LANES), 1)
    dest = jnp.zeros((DEST_TM, LANES), F32)
    for k in range(TOP_K):
        v = jnp.sum(jnp.where(lane_e == te[:, k:k + 1], ps_ref[...], 0.0), axis=-1, keepdims=True)
        dest = jnp.where(lane_k == k, v, dest)
    d_ref[...] = dest.astype(I32) + rank_ref[...]


def _dest_rows(te, rank, pstart):
    T = te.shape[0]
    row = pl.BlockSpec((DEST_TM, LANES), lambda i: (i, 0))
    return pl.pallas_call(
        _dest_kernel, out_shape=jax.ShapeDtypeStruct((T, LANES), I32), grid=(T // DEST_TM,),
        in_specs=[row, row, pl.BlockSpec((1, N_EXPERTS), lambda i: (0, 0))], out_specs=row,
        compiler_params=_cparams(("parallel",)), name="moe_dest",
    )(te, rank, pstart)


def _moe_ln(x1, xp, p, stacked, layer):
    T = x1.shape[0]
    te, gates, rank, counts = _router(x1, p['router_w'], p['router_bias'])
    counts = counts[0]
    padded = (counts + EXP_BLK - 1) // EXP_BLK * EXP_BLK
    pends = jnp.cumsum(padded)
    dest = _dest_rows(te, rank, (pends - padded).astype(F32)[None, :])[:, :TOP_K]
    nblk = (T * TOP_K + N_EXPERTS * (EXP_BLK - 1)) // EXP_BLK + 1
    tok = jnp.broadcast_to(jnp.arange(T, dtype=I32)[:, None], (T, TOP_K))
    row_tok = jnp.zeros((nblk * EXP_BLK,), I32).at[dest.reshape(-1)].set(
        tok.reshape(-1), unique_indices=True, mode='promise_in_bounds')
    blk_start = jnp.arange(nblk, dtype=I32) * EXP_BLK
    blk_e = jnp.minimum(jnp.sum((pends[None, :] <= blk_start[:, None]).astype(I32), axis=1),
                        N_EXPERTS - 1)
    n_used = (pends[-1] // EXP_BLK).astype(I32)[None]
    ys = _experts(xp.reshape(T, PACK_SUB, LANES), row_tok.reshape(nblk, EXP_BLK), blk_e, n_used,
                  stacked['e_gate'], stacked['e_up'], stacked['e_down'], layer)
    dest_km = dest.reshape(T // CMB_TOK, CMB_TOK, TOP_K).transpose(0, 2, 1).reshape(
        T // CMB_TOK, CMB_TOK * TOP_K)
    return _combine(ys, dest_km, x1, gates, p['s_gate'].astype(BF16), p['s_up'].astype(BF16),
                    p['s_down'].astype(BF16), p['ln2_g'][None, :], p['ln2_b'][None, :])


STACKED = ('e_gate', 'e_up', 'e_down')


def _encoder_layer(x, p, stacked, layer, lam_init, consts, B, L):
    ca, sa, cd, sd, gmat, hy_tabs = consts
    proj = _in_proj(x, p['w_in'].astype(BF16), ca, sa, cd, sd, L)
    lam = (jnp.exp(jnp.sum(p['lam_q1'] * p['lam_k1'])) - jnp.exp(jnp.sum(p['lam_q2'] * p['lam_k2']))
           + lam_init).astype(F32)[None]
    gn = (jnp.tile(p['a_subln_g'], A_HEADS) * (1.0 - lam_init))[None, :]
    o_a = _diff_attn(proj, lam, gn, gmat, B, L)
    o_b = _hyena(proj, p, hy_tabs, B, L)
    o_c = _na_attn(proj, _na_bias(p['na_rpb'], L // GRID_W), B, L)
    o_d = _win_attn(proj, p['d_sink'], B, L)
    x1, xp = _out_proj(o_a, o_b, o_c, o_d, p['w_out'].astype(BF16), x,
                       p['ln1_g'][None, :], p['ln1_b'][None, :])
    return _moe_ln(x1, xp, p, stacked, layer)


def _trunk(x3, params):
    B, L, _ = x3.shape
    ca, sa = _rope_tables(L, A_DH)
    cd, sd = _rope_tables(L, HEAD_DIM)
    head = jnp.arange(GROUP_W) // HEAD_DIM
    gmat = ((head[:, None] == head[None, :]).astype(F32) / HEAD_DIM).astype(BF16)
    hy_tabs = _trig_tables(L, False) + _trig_tables(L, True) + _hy_positions(L)
    consts = (ca, sa, cd, sd, gmat, hy_tabs)
    x = x3.reshape(B * L, D_MODEL)
    stacked = {name: params[name] for name in STACKED}
    for l in range(DEPTH):
        lp = {name: arr[l] for name, arr in params.items() if name not in STACKED}
        x = _encoder_layer(x, lp, stacked, l, 0.8 - 0.6 * math.exp(-0.3 * l), consts, B, L)
    return x.reshape(B, L, D_MODEL)


def kernel(x_prompt, x_sample, w_in, w_out, ln1_g, ln1_b, ln2_g, ln2_b, lam_q1, lam_k1, lam_q2,
           lam_k2, a_subln_g, hy_conv_w, hy_conv_b, hy_w1, hy_b1, hy_w2, hy_b2, hy_w3, hy_freq,
           hy_bias, na_rpb, d_sink, router_w, router_bias, e_gate, e_up, e_down, s_gate, s_up,
           s_down):
    params = {
        'w_in': w_in, 'w_out': w_out, 'ln1_g': ln1_g, 'ln1_b': ln1_b, 'ln2_g': ln2_g, 'ln2_b': ln2_b,
        'lam_q1': lam_q1, 'lam_k1': lam_k1, 'lam_q2': lam_q2, 'lam_k2': lam_k2,
        'a_subln_g': a_subln_g, 'hy_conv_w': hy_conv_w, 'hy_conv_b': hy_conv_b, 'hy_w1': hy_w1,
        'hy_b1': hy_b1, 'hy_w2': hy_w2, 'hy_b2': hy_b2, 'hy_w3': hy_w3, 'hy_freq': hy_freq,
        'hy_bias': hy_bias, 'na_rpb': na_rpb, 'd_sink': d_sink, 'router_w': router_w,
        'router_bias': router_bias, 'e_gate': e_gate, 'e_up': e_up, 'e_down': e_down,
        's_gate': s_gate, 's_up': s_up, 's_down': s_down,
    }
    return _trunk(x_prompt, params), _trunk(x_sample, params)
```

```python
import functools
import math

import jax
import jax.numpy as jnp
from jax import lax
from jax.experimental import pallas as pl
from jax.experimental.pallas import tpu as pltpu

F32 = jnp.float32
BF16 = jnp.bfloat16
I32 = jnp.int32

D_MODEL = 1024
DEPTH = 2
GRID_W = 64
HEAD_DIM = 64
GROUP_W = 256
A_HEADS = 4
A_DH = 32
B_CH = 256
HY_EMB = 33
HY_BANDS = 16
HY_FFN = 64
HY_INNER = 2
HY_TARGET = 1e-2
HY_FAST = 0.3
HY_SLOW = 1.5
C_HEADS = 4
NA_WR = 8
NA_WC = 16
D_HEADS = 4
D_KV_HEADS = 2
WIN = 128
D_IN = 2816
N_EXPERTS = 256
TOP_K = 8
D_EXPERT = 256
D_SHARED = 256
ROUTED_SCALE = 2.5
ROPE_THETA = 10000.0
LN_EPS = 1e-5
DN_ALPHA = (2 * DEPTH) ** 0.25
NEG = -1e30
LOG2E = 1.4426950408889634

LANES = 128
VMEM_LIMIT = 48 * 1024 * 1024

COL_AQ, COL_AK, COL_AV = 0, 1, 2
COL_B = 1
COL_CQ, COL_CK, COL_CV = 6, 7, 8
COL_DQ = 9
COL_DK, COL_DV = 20, 21

EXP_BLK = 256
CMB_TOK = 256


def _cparams(sem):
    return pltpu.CompilerParams(dimension_semantics=sem, vmem_limit_bytes=VMEM_LIMIT)


def _rope_lanes(y, cos, sin_signed, half):
    lane = lax.broadcasted_iota(I32, y.shape, 1)
    first = (lane % (2 * half)) < half
    rot = jnp.where(first, pltpu.roll(y, LANES - half, 1), pltpu.roll(y, half, 1))
    return y * cos + rot * sin_signed


def _in_proj_kernel(x_ref, w_ref, ca_ref, sa_ref, cd_ref, sd_ref, o_ref):
    xb = x_ref[...].astype(BF16)
    ca, sa, cd, sd = ca_ref[...], sa_ref[...], cd_ref[...], sd_ref[...]
    a_scale = (A_DH ** -0.5) * LOG2E
    hd_scale = (HEAD_DIM ** -0.5) * LOG2E
    for c in range(D_IN // 256):
        y = jnp.dot(xb, w_ref[:, c * 256:(c + 1) * 256], preferred_element_type=F32)
        for hh in range(2):
            ch = 2 * c + hh
            z = y[:, hh * LANES:(hh + 1) * LANES]
            if ch in (0, 1):
                z = _rope_lanes(z, ca, sa, A_DH // 2) * a_scale
            elif ch in (2, 3):
                z = _rope_lanes(z, ca, sa, A_DH // 2)
            elif ch in (12, 13):
                z = z * hd_scale
            elif ch in (18, 19):
                z = _rope_lanes(z, cd, sd, HEAD_DIM // 2) * hd_scale
            elif ch == 20:
                z = _rope_lanes(z, cd, sd, HEAD_DIM // 2)
            o_ref[:, ch * LANES:(ch + 1) * LANES] = z.astype(BF16)


def _in_proj(x, w_bf, ca, sa, cd, sd, L, tm=512):
    T = x.shape[0]
    nl = L // tm
    tab = pl.BlockSpec((tm, LANES), lambda i: (i % nl, 0))
    return pl.pallas_call(
        _in_proj_kernel,
        out_shape=jax.ShapeDtypeStruct((T, D_IN), BF16),
        grid=(T // tm,),
        in_specs=[pl.BlockSpec((tm, D_MODEL), lambda i: (i, 0)),
                  pl.BlockSpec((D_MODEL, D_IN), lambda i: (0, 0)),
                  tab, tab, tab, tab],
        out_specs=pl.BlockSpec((tm, D_IN), lambda i: (i, 0)),
        compiler_params=_cparams(("parallel",)),
        name="in_proj",
    )(x, w_bf, ca, sa, cd, sd)


def _rope_tables(L, dh):
    half = dh // 2
    lane = jnp.arange(LANES)
    inv_freq = ROPE_THETA ** (-(2.0 * (lane % half)).astype(F32) / dh)
    ang = jnp.arange(L, dtype=F32)[:, None] * inv_freq[None, :]
    sign = jnp.where((lane % dh) < half, -1.0, 1.0).astype(F32)
    return jnp.cos(ang), jnp.sin(ang) * sign[None, :]


def _diff_attn_kernel(lam_ref, q_ref, k_ref, v_ref, gn_ref, gm_ref, o_ref):
    lam = lam_ref[0]
    tq = q_ref.shape[0]
    q = q_ref[...]
    v = v_ref[...]
    lane = lax.broadcasted_iota(I32, (tq, LANES), 1)
    lane_o = lax.broadcasted_iota(I32, (tq, GROUP_W), 1)
    acc = jnp.zeros((tq, GROUP_W), F32)
    for h in range(A_HEADS):
        parts = []
        for c in range(2):
            comp = 2 * h + c
            j, off = comp // 4, (comp % 4) * A_DH
            qc = q[:, j * LANES:(j + 1) * LANES]
            qm = jnp.where((lane >= off) & (lane < off + A_DH), qc, jnp.zeros_like(qc))
            s = lax.dot_general(qm, k_ref[:, j * LANES:(j + 1) * LANES],
                                (((1,), (1,)), ((), ())), preferred_element_type=F32)
            m = jnp.max(s, axis=-1, keepdims=True)
            e = jnp.exp2(s - m)
            l = jnp.sum(e, axis=-1, keepdims=True)
            pv = jnp.dot(e.astype(BF16), v, preferred_element_type=F32)
            parts.append(pv * (1.0 / l))
        o_h = parts[0] - lam * parts[1]
        acc = jnp.where((lane_o >= h * HEAD_DIM) & (lane_o < (h + 1) * HEAD_DIM), o_h, acc)
    sq = acc * acc
    hi = sq.astype(BF16)
    lo = (sq - hi.astype(F32)).astype(BF16)
    ms = (jnp.dot(hi, gm_ref[...], preferred_element_type=F32)
          + jnp.dot(lo, gm_ref[...], preferred_element_type=F32))
    o_ref[...] = (acc * lax.rsqrt(ms + LN_EPS) * gn_ref[...]).astype(BF16)


def _diff_attn(proj, lam, gn, gmat, B, L, tq=256):
    nq = L // tq
    return pl.pallas_call(
        _diff_attn_kernel,
        out_shape=jax.ShapeDtypeStruct((B * L, GROUP_W), BF16),
        grid=(B, nq),
        in_specs=[pl.BlockSpec(memory_space=pltpu.SMEM),
                  pl.BlockSpec((tq, GROUP_W), lambda b, i: (b * nq + i, COL_AQ)),
                  pl.BlockSpec((L, GROUP_W), lambda b, i: (b, COL_AK)),
                  pl.BlockSpec((L, GROUP_W), lambda b, i: (b, COL_AV)),
                  pl.BlockSpec((1, GROUP_W), lambda b, i: (0, 0)),
                  pl.BlockSpec((GROUP_W, GROUP_W), lambda b, i: (0, 0))],
        out_specs=pl.BlockSpec((tq, GROUP_W), lambda b, i: (b * nq + i, 0)),
        compiler_params=_cparams(("parallel", "parallel")),
        name="diff_attn",
    )(lam, proj, proj, proj, gn, gmat)


HY_RC = 512


def _hy_prep_kernel(p_ref, w_ref, b_ref, u_ref, x0_ref):
    L = p_ref.shape[0]
    w0, w1, w2, bias = w_ref[0:1, :], w_ref[1:2, :], w_ref[2:3, :], b_ref[...]
    row = lax.broadcasted_iota(I32, (HY_RC, 3 * B_CH), 0)
    zero_row = jnp.zeros((1, 3 * B_CH), F32)
    for r in range(L // HY_RC):
        r0 = r * HY_RC
        xf = p_ref[r0:r0 + HY_RC, :].astype(F32)
        prev_row = zero_row if r == 0 else p_ref[r0 - 8:r0, :].astype(F32)[7:8, :]
        next_row = (zero_row if r0 + HY_RC == L
                    else p_ref[r0 + HY_RC:r0 + HY_RC + 8, :].astype(F32)[0:1, :])
        xp = jnp.where(row == 0, prev_row, pltpu.roll(xf, 1, 0))
        xn = jnp.where(row == HY_RC - 1, next_row, pltpu.roll(xf, HY_RC - 1, 0))
        y = bias + xp * w0 + xf * w1 + xn * w2
        x0, x1, v = y[:, :B_CH], y[:, B_CH:2 * B_CH], y[:, 2 * B_CH:]
        u_ref[r0:r0 + HY_RC, :] = (v * x1).astype(BF16)
        x0_ref[r0:r0 + HY_RC, :] = x0.astype(BF16)


def _hy_prep(proj, conv_w, conv_b, B, L):
    out = jax.ShapeDtypeStruct((B * L, B_CH), BF16)
    return pl.pallas_call(
        _hy_prep_kernel,
        out_shape=(out, out),
        grid=(B,),
        in_specs=[pl.BlockSpec((L, 3 * B_CH), lambda b: (b, COL_B)),
                  pl.BlockSpec((3, 3 * B_CH), lambda b: (0, 0)),
                  pl.BlockSpec((1, 3 * B_CH), lambda b: (0, 0))],
        out_specs=(pl.BlockSpec((L, B_CH), lambda b: (b, 0)),
                   pl.BlockSpec((L, B_CH), lambda b: (b, 0))),
        compiler_params=_cparams(("parallel",)),
        name="hy_prep",
    )(proj, conv_w, conv_b)


def _trig_table_kernel(ar_ref, ai_ref, br_ref, bi_ref, c_ref, s_ref):
    br, bi = br_ref[...], bi_ref[...]
    for th in range(c_ref.shape[1] // LANES):
        ar, ai = ar_ref[:, th:th + 1], ai_ref[:, th:th + 1]
        c_ref[:, th * LANES:(th + 1) * LANES] = (ar * br - ai * bi).astype(BF16)
        s_ref[:, th * LANES:(th + 1) * LANES] = (ar * bi + ai * br).astype(BF16)


def _trig_tables(L, transposed, tr=512):
    r = jnp.arange(L, dtype=I32)[:, None]
    hi = jnp.arange(L // LANES, dtype=I32)[None, :]
    lo = jnp.arange(LANES, dtype=I32)[None, :]
    if transposed:
        pa, pb = (2 * LANES * hi * r) % (4 * L), ((2 * lo + 1) * r) % (4 * L)
    else:
        pa, pb = ((2 * r + 1) * LANES * hi) % (4 * L), ((2 * r + 1) * lo) % (4 * L)
    ang_a = pa.astype(F32) * (math.pi / (2 * L))
    ang_b = pb.astype(F32) * (math.pi / (2 * L))
    out = jax.ShapeDtypeStruct((L, L), BF16)
    sa = pl.BlockSpec((tr, L // LANES), lambda i: (i, 0))
    sb = pl.BlockSpec((tr, LANES), lambda i: (i, 0))
    so = pl.BlockSpec((tr, L), lambda i: (i, 0))
    return pl.pallas_call(
        _trig_table_kernel, out_shape=(out, out), grid=(L // tr,),
        in_specs=[sa, sa, sb, sb], out_specs=(so, so),
        compiler_params=_cparams(("parallel",)), name="trig_tables",
    )(jnp.cos(ang_a), jnp.sin(ang_a), jnp.cos(ang_b), jnp.sin(ang_b))


def _hy_filter_kernel(z_ref, t_ref, m_ref, w1_ref, b1_ref, w2_ref, b2_ref, w3_ref, fr_ref,
                      ad_ref, k_ref):
    hp = lax.Precision.HIGHEST
    fr = fr_ref[...]
    h = jnp.sin(fr * (jnp.dot(z_ref[0], w1_ref[...], precision=hp, preferred_element_type=F32)
                      + b1_ref[...]))
    for i in range(HY_INNER):
        h = jnp.sin(fr * (jnp.dot(h, w2_ref[i], precision=hp, preferred_element_type=F32)
                          + b2_ref[i]))
    h = jnp.dot(h, w3_ref[...], precision=hp, preferred_element_type=F32)
    decay = jnp.exp(-t_ref[0] * ad_ref[...])
    half = pl.program_id(0)
    sel = jnp.where(half == 0, h[:, :B_CH], -h[:, B_CH:])
    k_ref[...] = (sel * decay * m_ref[0]).astype(BF16)


def _hy_filter(zz, tt, mm, w1p, b1, w2, b2, w3, freq, absdelta, L):
    full = lambda shape: pl.BlockSpec(shape, lambda h: (0,) * len(shape))
    return pl.pallas_call(
        _hy_filter_kernel,
        out_shape=jax.ShapeDtypeStruct((L, 2 * B_CH), BF16),
        grid=(2,),
        in_specs=[pl.BlockSpec((1, L, LANES), lambda h: (h, 0, 0)),
                  pl.BlockSpec((1, L, 1), lambda h: (h, 0, 0)),
                  pl.BlockSpec((1, L, 1), lambda h: (h, 0, 0)),
                  full((LANES, HY_FFN)), full((1, HY_FFN)),
                  full((HY_INNER, HY_FFN, HY_FFN)), full((HY_INNER, 1, HY_FFN)),
                  full((HY_FFN, 2 * B_CH)), full((1, HY_FFN)), full((1, B_CH))],
        out_specs=pl.BlockSpec((L, B_CH), lambda h: (0, h)),
        compiler_params=_cparams(("parallel",)),
        name="hy_filter",
    )(zz, tt, mm, w1p, b1, w2, b2, w3, freq, absdelta)


def _dft_fwd_kernel(c_ref, s_ref, x_ref, oc_ref, os_ref):
    x = x_ref[...]
    oc_ref[...] = jnp.dot(c_ref[...], x, preferred_element_type=F32)
    os_ref[...] = jnp.dot(s_ref[...], x, preferred_element_type=F32)


def _dft_fwd(ctab, stab, x, L, tf=512):
    n = x.shape[1]
    out = jax.ShapeDtypeStruct((L, n), F32)
    return pl.pallas_call(
        _dft_fwd_kernel, out_shape=(out, out), grid=(L // tf,),
        in_specs=[pl.BlockSpec((tf, L), lambda i: (i, 0)),
                  pl.BlockSpec((tf, L), lambda i: (i, 0)),
                  pl.BlockSpec((L, n), lambda i: (0, 0))],
        out_specs=(pl.BlockSpec((tf, n), lambda i: (i, 0)),
                   pl.BlockSpec((tf, n), lambda i: (i, 0))),
        compiler_params=_cparams(("parallel",)), name="dft_filter",
    )(ctab, stab, x)


def _dft_prod_kernel(c_ref, s_ref, u_ref, hc_ref, hs_ref, yr_ref, yi_ref):
    u = u_ref[...]
    uc = jnp.dot(c_ref[...], u, preferred_element_type=F32)
    us = jnp.dot(s_ref[...], u, preferred_element_type=F32)
    tf = uc.shape[0]
    par = lax.broadcasted_iota(I32, (tf, B_CH), 0) % 2
    sgn = jnp.where(par == 0, 1.0, -1.0).astype(F32)
    hc, hs = hc_ref[...], hs_ref[...]
    kc = hc[:, :B_CH] - sgn * hs[:, B_CH:]
    ks = hs[:, :B_CH] + sgn * hc[:, B_CH:]
    yr_ref[...] = (uc * kc - us * ks).astype(BF16)
    yi_ref[...] = (uc * ks + us * kc).astype(BF16)


def _dft_prod(ctab, stab, u, hc, hs, B, L, tf=512):
    nf = L // tf
    out = jax.ShapeDtypeStruct((B * L, B_CH), BF16)
    return pl.pallas_call(
        _dft_prod_kernel, out_shape=(out, out), grid=(nf, B),
        in_specs=[pl.BlockSpec((tf, L), lambda f, b: (f, 0)),
                  pl.BlockSpec((tf, L), lambda f, b: (f, 0)),
                  pl.BlockSpec((L, B_CH), lambda f, b: (b, 0)),
                  pl.BlockSpec((tf, 2 * B_CH), lambda f, b: (f, 0)),
                  pl.BlockSpec((tf, 2 * B_CH), lambda f, b: (f, 0))],
        out_specs=(pl.BlockSpec((tf, B_CH), lambda f, b: (b * nf + f, 0)),
                   pl.BlockSpec((tf, B_CH), lambda f, b: (b * nf + f, 0))),
        compiler_params=_cparams(("parallel", "parallel")), name="dft_fwd_prod",
    )(ctab, stab, u, hc, hs)


def _dft_inv_kernel(ct_ref, st_ref, yr_ref, yi_ref, u_ref, x0_ref, bias_ref, o_ref, *, inv_l):
    y = (jnp.dot(ct_ref[...], yr_ref[...], preferred_element_type=F32)
         + jnp.dot(st_ref[...], yi_ref[...], preferred_element_type=F32)) * inv_l
    v = y + u_ref[...].astype(F32) * bias_ref[...]
    o_ref[...] = (v * x0_ref[...].astype(F32)).astype(BF16)


def _dft_inv(ctt, stt, yr, yi, u, x0, bias, B, L, tt=512):
    nt = L // tt
    return pl.pallas_call(
        functools.partial(_dft_inv_kernel, inv_l=1.0 / L),
        out_shape=jax.ShapeDtypeStruct((B * L, B_CH), BF16), grid=(nt, B),
        in_specs=[pl.BlockSpec((tt, L), lambda t, b: (t, 0)),
                  pl.BlockSpec((tt, L), lambda t, b: (t, 0)),
                  pl.BlockSpec((L, B_CH), lambda t, b: (b, 0)),
                  pl.BlockSpec((L, B_CH), lambda t, b: (b, 0)),
                  pl.BlockSpec((tt, B_CH), lambda t, b: (b * nt + t, 0)),
                  pl.BlockSpec((tt, B_CH), lambda t, b: (b * nt + t, 0)),
                  pl.BlockSpec((1, B_CH), lambda t, b: (0, 0))],
        out_specs=pl.BlockSpec((tt, B_CH), lambda t, b: (b * nt + t, 0)),
        compiler_params=_cparams(("parallel", "parallel")), name="dft_inv",
    )(ctt, stt, yr, yi, u, x0, bias)


def _hy_positions(L):
    t = jnp.linspace(0.0, 1.0, L, dtype=F32)[:, None]
    w = 2.0 * math.pi * jnp.arange(L, dtype=F32)[:, None] / L
    f = jnp.linspace(1e-4, HY_BANDS - 1, HY_BANDS, dtype=F32)[None, :]
    z = jnp.concatenate([t, jnp.cos(f * w), -jnp.sin(f * w)], axis=-1)

    def mirrored(a):
        return jnp.concatenate([a[:1], a[:0:-1]], axis=0)

    zz = jnp.pad(jnp.stack([z, mirrored(z)]), ((0, 0), (0, 0), (0, LANES - HY_EMB)))
    tt = jnp.stack([t, mirrored(t)])
    mm = jnp.ones((2, L, 1), F32).at[1, 0, 0].set(0.0)
    return zz, tt, mm


def _hyena(proj, p, tabs, B, L):
    ctab, stab, ctt, stt, zz, tt, mm = tabs
    u, x0 = _hy_prep(proj, p['hy_conv_w'], p['hy_conv_b'][None, :], B, L)
    max_decay = math.log(HY_TARGET) / HY_FAST
    min_decay = math.log(HY_TARGET) / HY_SLOW
    absdelta = jnp.abs(jnp.linspace(min_decay, max_decay, B_CH, dtype=F32))[None, :]
    w1p = jnp.pad(p['hy_w1'], ((0, LANES - HY_EMB), (0, 0)))
    k2 = _hy_filter(zz, tt, mm, w1p, p['hy_b1'][None, :], p['hy_w2'], p['hy_b2'][:, None, :],
                    p['hy_w3'], p['hy_freq'][None, :], absdelta, L)
    hc, hs = _dft_fwd(ctab, stab, k2, L)
    yr, yi = _dft_prod(ctab, stab, u, hc, hs, B, L)
    return _dft_inv(ctt, stt, yr, yi, u, x0, p['hy_bias'][None, :], B, L)


def _na_kernel(var_ref, q_ref, k_ref, v_ref, bias_ref, o_ref, *, rows):
    r = pl.program_id(1)
    wr = min(NA_WR, rows)
    kr0 = jnp.clip(r - wr // 2, 0, rows - wr)
    start = pl.multiple_of(kr0 * GRID_W, GRID_W)
    nk = wr * GRID_W
    q = q_ref[...]
    k = k_ref[pl.ds(start, nk), :]
    v = v_ref[pl.ds(start, nk), :]
    lane = lax.broadcasted_iota(I32, (GRID_W, LANES), 1)
    lane_o = lax.broadcasted_iota(I32, (GRID_W, GROUP_W), 1)
    qi = lax.broadcasted_iota(I32, (GRID_W, nk), 0)
    kc = lax.broadcasted_iota(I32, (GRID_W, nk), 1) % GRID_W
    c0 = jnp.clip(qi - NA_WC // 2, 0, GRID_W - NA_WC)
    ok = (kc >= c0) & (kc < c0 + NA_WC)
    acc = jnp.zeros((GRID_W, GROUP_W), F32)
    for h in range(C_HEADS):
        j, off = h // 2, (h % 2) * HEAD_DIM
        qc = q[:, j * LANES:(j + 1) * LANES]
        qm = jnp.where((lane >= off) & (lane < off + HEAD_DIM), qc, jnp.zeros_like(qc))
        s = lax.dot_general(qm, k[:, j * LANES:(j + 1) * LANES], (((1,), (1,)), ((), ())),
                            preferred_element_type=F32)
        s = jnp.where(ok, s + bias_ref[0, h], NEG)
        m = jnp.max(s, axis=-1, keepdims=True)
        e = jnp.exp2(s - m)
        l = jnp.sum(e, axis=-1, keepdims=True)
        pv = jnp.dot(e.astype(BF16), v, preferred_element_type=F32) * (1.0 / l)
        acc = jnp.where((lane_o >= h * HEAD_DIM) & (lane_o < (h + 1) * HEAD_DIM), pv, acc)
    o_ref[...] = acc.astype(BF16)


def _na_bias(rpb, rows):
    wr = min(NA_WR, rows)
    c = jnp.arange(GRID_W)
    dc = jnp.clip(c[None, :] - c[:, None], 1 - NA_WC, NA_WC - 1) + (NA_WC - 1)
    onehot = (dc[None, :, :] == jnp.arange(2 * NA_WC - 1)[:, None, None]).astype(F32)
    cols = jnp.einsum('hrd,dqk->hrqk', rpb.astype(F32) * LOG2E, onehot,
                      precision=lax.Precision.HIGHEST)
    variants = []
    for i in range(NA_WR):
        blk = cols[:, NA_WR - 1 - i:NA_WR - 1 - i + wr]
        variants.append(jnp.moveaxis(blk, 1, 2).reshape(C_HEADS, GRID_W, wr * GRID_W))
    return jnp.stack(variants)


def _na_attn(proj, bias, B, L):
    rows = L // GRID_W
    wr = min(NA_WR, rows)
    r = jnp.arange(rows)
    variant = (r - jnp.clip(r - wr // 2, 0, rows - wr)).astype(I32)
    nk = wr * GRID_W
    grid_spec = pltpu.PrefetchScalarGridSpec(
        num_scalar_prefetch=1, grid=(B, rows),
        in_specs=[pl.BlockSpec((GRID_W, GROUP_W), lambda b, r, var: (b * rows + r, COL_CQ)),
                  pl.BlockSpec((L, GROUP_W), lambda b, r, var: (b, COL_CK)),
                  pl.BlockSpec((L, GROUP_W), lambda b, r, var: (b, COL_CV)),
                  pl.BlockSpec((1, C_HEADS, GRID_W, nk), lambda b, r, var: (var[r], 0, 0, 0))],
        out_specs=pl.BlockSpec((GRID_W, GROUP_W), lambda b, r, var: (b * rows + r, 0)))
    return pl.pallas_call(
        functools.partial(_na_kernel, rows=rows),
        out_shape=jax.ShapeDtypeStruct((B * L, GROUP_W), BF16),
        grid_spec=grid_spec,
        compiler_params=_cparams(("parallel", "arbitrary")),
        name="na_attn",
    )(variant, proj, proj, proj, bias)


WQ_TILE = 256
WK_SPAN = WQ_TILE + 2 * WIN


def _win_kernel(sink_ref, q_ref, k_ref, v_ref, o_ref, *, L):
    i = pl.program_id(1)
    q0 = i * WQ_TILE
    k0 = pl.multiple_of(jnp.clip(q0 - WIN, 0, L - WK_SPAN), WIN)
    k = k_ref[pl.ds(k0, WK_SPAN), :]
    v = v_ref[pl.ds(k0, WK_SPAN), :]
    qf = q_ref[...].astype(F32)
    qpos = q0 + lax.broadcasted_iota(I32, (WQ_TILE, WK_SPAN), 0)
    kpos = k0 + lax.broadcasted_iota(I32, (WQ_TILE, WK_SPAN), 1)
    ok = jnp.abs(qpos - kpos) <= WIN
    lane = lax.broadcasted_iota(I32, (WQ_TILE, LANES), 1)
    for j in range(2):
        out = jnp.zeros((WQ_TILE, LANES), F32)
        for hh in range(2):
            h = 2 * j + hh
            g = h // (D_HEADS // D_KV_HEADS)
            qc = qf[:, j * LANES:(j + 1) * LANES]
            if hh != g:
                qc = pltpu.roll(qc, HEAD_DIM, 1)
            in_g = (lane >= g * HEAD_DIM) & (lane < (g + 1) * HEAD_DIM)
            qm = jnp.where(in_g, qc, 0.0).astype(BF16)
            s = lax.dot_general(qm, k, (((1,), (1,)), ((), ())), preferred_element_type=F32)
            s = jnp.where(ok, s, NEG)
            sk = sink_ref[h] * LOG2E
            m = jnp.maximum(jnp.max(s, axis=-1, keepdims=True), sk)
            e = jnp.exp2(s - m)
            l = jnp.sum(e, axis=-1, keepdims=True) + jnp.exp2(sk - m)
            pv = jnp.dot(e.astype(BF16), v, preferred_element_type=F32) * (1.0 / l)
            pv = jnp.where(in_g, pv, 0.0)
            if hh != g:
                pv = pltpu.roll(pv, HEAD_DIM, 1)
            out = out + pv
        o_ref[:, j * LANES:(j + 1) * LANES] = out.astype(BF16)


def _win_attn(proj, sink, B, L):
    nq = L // WQ_TILE
    return pl.pallas_call(
        functools.partial(_win_kernel, L=L),
        out_shape=jax.ShapeDtypeStruct((B * L, GROUP_W), BF16),
        grid=(B, nq),
        in_specs=[pl.BlockSpec(memory_space=pltpu.SMEM),
                  pl.BlockSpec((WQ_TILE, GROUP_W), lambda b, i: (b * nq + i, COL_DQ)),
                  pl.BlockSpec((L, LANES), lambda b, i: (b, COL_DK)),
                  pl.BlockSpec((L, LANES), lambda b, i: (b, COL_DV))],
        out_specs=pl.BlockSpec((WQ_TILE, GROUP_W), lambda b, i: (b * nq + i, 0)),
        compiler_params=_cparams(("parallel", "parallel")),
        name="win_attn",
    )(sink, proj, proj, proj)


def _layer_norm(y, g, b):
    mu = jnp.mean(y, axis=-1, keepdims=True)
    d = y - mu
    var = jnp.mean(d * d, axis=-1, keepdims=True)
    return d * lax.rsqrt(var + LN_EPS) * g + b


PACK_W = D_MODEL // 2
HI_MASK = -65536


def _pack_bf16_pairs(y):
    lo = lax.bitcast_convert_type(y[:, :PACK_W].astype(BF16).astype(F32), I32)
    hi = lax.bitcast_convert_type(y[:, PACK_W:].astype(BF16).astype(F32), I32)
    return lax.shift_right_logical(lo, 16) | (hi & HI_MASK)


def _unpack_bf16_pairs(w):
    return (lax.bitcast_convert_type(w << 16, F32), lax.bitcast_convert_type(w & HI_MASK, F32))


def _out_proj_kernel(oa_ref, ob_ref, oc_ref, od_ref, w_ref, x_ref, g_ref, b_ref, o_ref, p_ref):
    acc = DN_ALPHA * x_ref[...]
    for gi, ref in enumerate((oa_ref, ob_ref, oc_ref, od_ref)):
        acc = acc + jnp.dot(ref[...], w_ref[gi * GROUP_W:(gi + 1) * GROUP_W, :],
                            preferred_element_type=F32)
    y = _layer_norm(acc, g_ref[...], b_ref[...])
    o_ref[...] = y
    p_ref[...] = _pack_bf16_pairs(y)


def _out_proj(oa, ob, oc, od, w_bf, x, g, b, tm=512):
    T = x.shape[0]
    og = pl.BlockSpec((tm, GROUP_W), lambda i: (i, 0))
    row = pl.BlockSpec((tm, D_MODEL), lambda i: (i, 0))
    vec = pl.BlockSpec((1, D_MODEL), lambda i: (0, 0))
    return pl.pallas_call(
        _out_proj_kernel,
        out_shape=(jax.ShapeDtypeStruct((T, D_MODEL), F32), jax.ShapeDtypeStruct((T, PACK_W), I32)),
        grid=(T // tm,),
        in_specs=[og, og, og, og, pl.BlockSpec((D_MODEL, D_MODEL), lambda i: (0, 0)), row, vec, vec],
        out_specs=(row, pl.BlockSpec((tm, PACK_W), lambda i: (i, 0))),
        compiler_params=_cparams(("parallel",)),
        name="out_proj_ln",
    )(oa, ob, oc, od, w_bf, x, g, b)


RT_TM = 256


def _router_kernel(x_ref, wh_ref, wl_ref, bias_ref, tri_ref, te_ref, gate_ref, rank_ref, cnt_ref,
                   carry):
    @pl.when(pl.program_id(0) == 0)
    def _():
        carry[...] = jnp.zeros_like(carry)

    x = x_ref[...]
    xh = x.astype(BF16)
    xl = (x - xh.astype(F32)).astype(BF16)
    logits = (jnp.dot(xh, wh_ref[...], preferred_element_type=F32)
              + jnp.dot(xh, wl_ref[...], preferred_element_type=F32)
              + jnp.dot(xl, wh_ref[...], preferred_element_type=F32))
    scores = 1.0 / (1.0 + jnp.exp(-logits))
    sel = scores + bias_ref[...]
    lane_e = lax.broadcasted_iota(I32, (RT_TM, N_EXPERTS), 1).astype(F32)
    lane_k = lax.broadcasted_iota(I32, (RT_TM, LANES), 1)
    te = jnp.zeros((RT_TM, LANES), F32)
    ts = jnp.zeros((RT_TM, LANES), F32)
    onehot = jnp.zeros((RT_TM, N_EXPERTS), F32)
    for k in range(TOP_K):
        m = jnp.max(sel, axis=-1, keepdims=True)
        idx = jnp.min(jnp.where(sel == m, lane_e, float(N_EXPERTS)), axis=-1, keepdims=True)
        hit = lane_e == idx
        sc = jnp.sum(jnp.where(hit, scores, 0.0), axis=-1, keepdims=True)
        te = jnp.where(lane_k == k, idx, te)
        ts = jnp.where(lane_k == k, sc, ts)
        onehot = jnp.where(hit, 1.0, onehot)
        sel = jnp.where(hit, -jnp.inf, sel)
    gate_ref[...] = ts / jnp.sum(ts, axis=-1, keepdims=True) * ROUTED_SCALE
    te_ref[...] = te.astype(I32)
    before = jnp.dot(tri_ref[...], onehot.astype(BF16), preferred_element_type=F32) + carry[...]
    rank = jnp.zeros((RT_TM, LANES), F32)
    for k in range(TOP_K):
        rk = jnp.sum(jnp.where(lane_e == te[:, k:k + 1], before, 0.0), axis=-1, keepdims=True)
        rank = jnp.where(lane_k == k, rk, rank)
    rank_ref[...] = rank.astype(I32)
    carry[...] = carry[...] + jnp.sum(onehot, axis=0, keepdims=True)
    cnt_ref[...] = carry[...].astype(I32)


def _router(x1, wr, bias):
    T = x1.shape[0]
    wh = wr.astype(BF16)
    wl = (wr - wh.astype(F32)).astype(BF16)
    tri = (jnp.arange(RT_TM)[:, None] > jnp.arange(RT_TM)[None, :]).astype(BF16)
    row = pl.BlockSpec((RT_TM, LANES), lambda i: (i, 0))
    full = lambda shape: pl.BlockSpec(shape, lambda i: (0, 0))
    return pl.pallas_call(
        _router_kernel,
        out_shape=(jax.ShapeDtypeStruct((T, LANES), I32), jax.ShapeDtypeStruct((T, LANES), F32),
                   jax.ShapeDtypeStruct((T, LANES), I32), jax.ShapeDtypeStruct((1, N_EXPERTS), I32)),
        grid=(T // RT_TM,),
        in_specs=[pl.BlockSpec((RT_TM, D_MODEL), lambda i: (i, 0)),
                  full((D_MODEL, N_EXPERTS)), full((D_MODEL, N_EXPERTS)), full((1, N_EXPERTS)),
                  full((RT_TM, RT_TM))],
        out_specs=(row, row, row, full((1, N_EXPERTS))),
        scratch_shapes=[pltpu.VMEM((1, N_EXPERTS), F32)],
        compiler_params=_cparams(("arbitrary",)),
        name="router_topk",
    )(x1, wh, wl, bias[None, :], tri)


def _idx_copy(idx_hbm, idx_smem, isem, blk, slot):
    return pltpu.make_async_copy(idx_hbm.at[blk], idx_smem.at[slot], isem.at[slot])


ROW_SUB = D_MODEL // LANES


IDX_SLOTS = 3


def _row_copy(src_hbm, buf, sem, idx_smem, islot, bslot, r):
    src = pl.multiple_of(idx_smem[islot, r] * ROW_SUB, ROW_SUB)
    return pltpu.make_async_copy(src_hbm.at[pl.ds(src, ROW_SUB), :],
                                 buf.at[bslot, pl.ds(r * ROW_SUB, ROW_SUB), :], sem.at[bslot])


def _tile_rows_chunk(buf_slot, row0, n_rows, c):
    return buf_slot[pl.ds(row0 * ROW_SUB + c, n_rows, stride=ROW_SUB), :]


def _gather_step(i, n, idx_hbm, src_hbm, idx_smem, buf, isem, sem, n_rows):
    def issue_rows(islot, bslot):
        for r in range(n_rows):
            _row_copy(src_hbm, buf, sem, idx_smem, islot, bslot, r).start()

    @pl.when(i == 0)
    def _():
        _idx_copy(idx_hbm, idx_smem, isem, 0, 0).start()
        _idx_copy(idx_hbm, idx_smem, isem, 0, 0).wait()
        issue_rows(0, 0)

        @pl.when(n > 1)
        def _():
            _idx_copy(idx_hbm, idx_smem, isem, 1, 1).start()

    @pl.when(i + 2 < n)
    def _():
        _idx_copy(idx_hbm, idx_smem, isem, i + 2, (i + 2) % IDX_SLOTS).start()

    cur = i % 2

    @pl.when(i + 1 < n)
    def _():
        _idx_copy(idx_hbm, idx_smem, isem, i + 1, (i + 1) % IDX_SLOTS).wait()
        issue_rows((i + 1) % IDX_SLOTS, 1 - cur)

    for r in range(n_rows):
        _row_copy(src_hbm, buf, sem, idx_smem, i % IDX_SLOTS, cur, r).wait()

    return cur


PACK_SUB = PACK_W // LANES
EXP_VMEM_LIMIT = 56 * 1024 * 1024


def _expert_kernel(blk_e_ref, nused_ref, tok_hbm, xp_hbm, wg_ref, wu_ref, wd_ref, y_ref,
                   idx_smem, xp_vmem, gbuf, wgb, wub, wdb, isem, xsem):
    i = pl.program_id(0)
    n = pl.num_programs(0)

    def idx_copy(blk, slot):
        return pltpu.make_async_copy(tok_hbm.at[blk], idx_smem.at[slot], isem.at[slot])

    @pl.when(i == 0)
    def _():
        load_all = pltpu.make_async_copy(xp_hbm, xp_vmem, xsem)
        load_all.start()
        idx_copy(0, 0).start()
        load_all.wait()

    cur = i % 2
    idx_copy(i, cur).wait()

    @pl.when(i + 1 < n)
    def _():
        idx_copy(i + 1, 1 - cur).start()

    prev_e = blk_e_ref[jnp.maximum(i - 1, 0)]

    @pl.when((i == 0) | (blk_e_ref[i] != prev_e))
    def _():
        wgb[...] = wg_ref[0, 0].astype(BF16)
        wub[...] = wu_ref[0, 0].astype(BF16)
        wdb[...] = wd_ref[0, 0].astype(BF16)

    @pl.when(i < nused_ref[0])
    def _():
        for r in range(EXP_BLK):
            gbuf[pl.ds(r * PACK_SUB, PACK_SUB), :] = xp_vmem[idx_smem[cur, r]]
        halves = [_unpack_bf16_pairs(gbuf[pl.ds(c, EXP_BLK, stride=PACK_SUB), :])
                  for c in range(PACK_SUB)]
        xb = jnp.concatenate([h[0] for h in halves] + [h[1] for h in halves], axis=1).astype(BF16)
        g = jnp.dot(xb, wgb[...], preferred_element_type=F32)
        u = jnp.dot(xb, wub[...], preferred_element_type=F32)
        h = (g * (1.0 / (1.0 + jnp.exp(-g))) * u).astype(BF16)
        y = jnp.dot(h, wdb[...], preferred_element_type=F32)
        for c in range(ROW_SUB):
            y_ref[pl.ds(c, EXP_BLK, stride=ROW_SUB), :] = y[:, c * LANES:(c + 1) * LANES]

    @pl.when(i >= nused_ref[0])
    def _():
        y_ref[...] = jnp.zeros_like(y_ref)


def _experts(xp, row_tok, blk_e, n_used, e_gate, e_up, e_down, layer):
    nblk = row_tok.shape[0]
    wmap = lambda i, be, nu: (layer, be[i], 0, 0)
    grid_spec = pltpu.PrefetchScalarGridSpec(
        num_scalar_prefetch=2, grid=(nblk,),
        in_specs=[pl.BlockSpec(memory_space=pl.ANY),
                  pl.BlockSpec(memory_space=pl.ANY),
                  pl.BlockSpec((1, 1, D_MODEL, D_EXPERT), wmap),
                  pl.BlockSpec((1, 1, D_MODEL, D_EXPERT), wmap),
                  pl.BlockSpec((1, 1, D_EXPERT, D_MODEL), wmap)],
        out_specs=pl.BlockSpec((EXP_BLK * ROW_SUB, LANES), lambda i, be, nu: (i, 0)),
        scratch_shapes=[pltpu.SMEM((2, EXP_BLK), I32),
                        pltpu.VMEM(xp.shape, I32),
                        pltpu.VMEM((PACK_SUB * EXP_BLK, LANES), I32),
                        pltpu.VMEM((D_MODEL, D_EXPERT), BF16),
                        pltpu.VMEM((D_MODEL, D_EXPERT), BF16),
                        pltpu.VMEM((D_EXPERT, D_MODEL), BF16),
                        pltpu.SemaphoreType.DMA((2,)),
                        pltpu.SemaphoreType.DMA])
    return pl.pallas_call(
        _expert_kernel,
        out_shape=jax.ShapeDtypeStruct((nblk * EXP_BLK * ROW_SUB, LANES), F32),
        grid_spec=grid_spec,
        compiler_params=pltpu.CompilerParams(dimension_semantics=("arbitrary",),
                                             vmem_limit_bytes=EXP_VMEM_LIMIT),
        name="moe_experts",
    )(blk_e, n_used, row_tok, xp, e_gate, e_up, e_down)


def _combine_kernel(dest_hbm, ys_hbm, x_ref, gate_ref, sg_ref, su_ref, sd_ref, g_ref, b_ref, o_ref,
                    idx_smem, buf, isem, sem):
    i = pl.program_id(0)
    n = pl.num_programs(0)
    slot = _gather_step(i, n, dest_hbm, ys_hbm, idx_smem, buf, isem, sem, CMB_TOK * TOP_K)
    x = x_ref[...]
    xb = x.astype(BF16)
    g = jnp.dot(xb, sg_ref[...], preferred_element_type=F32)
    u = jnp.dot(xb, su_ref[...], preferred_element_type=F32)
    h = (g * (1.0 / (1.0 + jnp.exp(-g))) * u).astype(BF16)
    acc = DN_ALPHA * x + jnp.dot(h, sd_ref[...], preferred_element_type=F32)
    gate = gate_ref[...]
    chunks = []
    for c in range(ROW_SUB):
        part = jnp.zeros((CMB_TOK, LANES), F32)
        for k in range(TOP_K):
            part = part + _tile_rows_chunk(buf.at[slot], k * CMB_TOK, CMB_TOK, c) * gate[:, k:k + 1]
        chunks.append(part)
    acc = acc + jnp.concatenate(chunks, axis=1)
    o_ref[...] = _layer_norm(acc, g_ref[...], b_ref[...])


def _combine(ys, dest_km, x1, gates, sg, su, sd, g, b):
    T = x1.shape[0]
    full = lambda shape: pl.BlockSpec(shape, lambda i: (0, 0))
    row = pl.BlockSpec((CMB_TOK, D_MODEL), lambda i: (i, 0))
    return pl.pallas_call(
        _combine_kernel,
        out_shape=jax.ShapeDtypeStruct((T, D_MODEL), F32),
        grid=(T // CMB_TOK,),
        in_specs=[pl.BlockSpec(memory_space=pl.ANY), pl.BlockSpec(memory_space=pl.ANY),
                  row, pl.BlockSpec((CMB_TOK, LANES), lambda i: (i, 0)),
                  full((D_MODEL, D_SHARED)), full((D_MODEL, D_SHARED)), full((D_SHARED, D_MODEL)),
                  full((1, D_MODEL)), full((1, D_MODEL))],
        out_specs=row,
        scratch_shapes=[pltpu.SMEM((IDX_SLOTS, CMB_TOK * TOP_K), I32),
                        pltpu.VMEM((2, CMB_TOK * TOP_K * ROW_SUB, LANES), F32),
                        pltpu.SemaphoreType.DMA((IDX_SLOTS,)),
                        pltpu.SemaphoreType.DMA((2,))],
        compiler_params=_cparams(("arbitrary",)),
        name="moe_combine_ln",
    )(dest_km, ys, x1, gates, sg, su, sd, g, b)


DEST_TM = 1024


def _dest_kernel(te_ref, rank_ref, ps_ref, d_ref):
    te = te_ref[...]
    lane_e = lax.broadcasted_iota(I32, (DEST_TM, N_EXPERTS), 1)
    lane_k = lax.broadcasted_iota(I32, (DEST_TM, LANES), 1)
    dest = jnp.zeros((DEST_TM, LANES), F32)
    for k in range(TOP_K):
        v = jnp.sum(jnp.where(lane_e == te[:, k:k + 1], ps_ref[...], 0.0), axis=-1, keepdims=True)
        dest = jnp.where(lane_k == k, v, dest)
    d_ref[...] = dest.astype(I32) + rank_ref[...]


def _dest_rows(te, rank, pstart):
    T = te.shape[0]
    row = pl.BlockSpec((DEST_TM, LANES), lambda i: (i, 0))
    return pl.pallas_call(
        _dest_kernel, out_shape=jax.ShapeDtypeStruct((T, LANES), I32), grid=(T // DEST_TM,),
        in_specs=[row, row, pl.BlockSpec((1, N_EXPERTS), lambda i: (0, 0))], out_specs=row,
        compiler_params=_cparams(("parallel",)), name="moe_dest",
    )(te, rank, pstart)


def _moe_ln(x1, xp, p, stacked, layer):
    T = x1.shape[0]
    te, gates, rank, counts = _router(x1, p['router_w'], p['router_bias'])
    counts = counts[0]
    padded = (counts + EXP_BLK - 1) // EXP_BLK * EXP_BLK
    pends = jnp.cumsum(padded)
    dest = _dest_rows(te, rank, (pends - padded).astype(F32)[None, :])[:, :TOP_K]
    nblk = (T * TOP_K + N_EXPERTS * (EXP_BLK - 1)) // EXP_BLK + 1
    tok = jnp.broadcast_to(jnp.arange(T, dtype=I32)[:, None], (T, TOP_K))
    row_tok = jnp.zeros((nblk * EXP_BLK,), I32).at[dest.reshape(-1)].set(
        tok.reshape(-1), unique_indices=True, mode='promise_in_bounds')
    blk_start = jnp.arange(nblk, dtype=I32) * EXP_BLK
    blk_e = jnp.minimum(jnp.sum((pends[None, :] <= blk_start[:, None]).astype(I32), axis=1),
                        N_EXPERTS - 1)
    n_used = (pends[-1] // EXP_BLK).astype(I32)[None]
    ys = _experts(xp.reshape(T, PACK_SUB, LANES), row_tok.reshape(nblk, EXP_BLK), blk_e, n_used,
                  stacked['e_gate'], stacked['e_up'], stacked['e_down'], layer)
    dest_km = dest.reshape(T // CMB_TOK, CMB_TOK, TOP_K).transpose(0, 2, 1).reshape(
        T // CMB_TOK, CMB_TOK * TOP_K)
    return _combine(ys, dest_km, x1, gates, p['s_gate'].astype(BF16), p['s_up'].astype(BF16),
                    p['s_down'].astype(BF16), p['ln2_g'][None, :], p['ln2_b'][None, :])


STACKED = ('e_gate', 'e_up', 'e_down')


def _encoder_layer(x, p, stacked, layer, lam_init, consts, B, L):
    ca, sa, cd, sd, gmat, hy_tabs = consts
    proj = _in_proj(x, p['w_in'].astype(BF16), ca, sa, cd, sd, L)
    lam = (jnp.exp(jnp.sum(p['lam_q1'] * p['lam_k1'])) - jnp.exp(jnp.sum(p['lam_q2'] * p['lam_k2']))
           + lam_init).astype(F32)[None]
    gn = (jnp.tile(p['a_subln_g'], A_HEADS) * (1.0 - lam_init))[None, :]
    o_a = _diff_attn(proj, lam, gn, gmat, B, L)
    o_b = _hyena(proj, p, hy_tabs, B, L)
    o_c = _na_attn(proj, _na_bias(p['na_rpb'], L // GRID_W), B, L)
    o_d = _win_attn(proj, p['d_sink'], B, L)
    x1, xp = _out_proj(o_a, o_b, o_c, o_d, p['w_out'].astype(BF16), x,
                       p['ln1_g'][None, :], p['ln1_b'][None, :])
    return _moe_ln(x1, xp, p, stacked, layer)


def _trunk(x3, params):
    B, L, _ = x3.shape
    ca, sa = _rope_tables(L, A_DH)
    cd, sd = _rope_tables(L, HEAD_DIM)
    head = jnp.arange(GROUP_W) // HEAD_DIM
    gmat = ((head[:, None] == head[None, :]).astype(F32) / HEAD_DIM).astype(BF16)
    hy_tabs = _trig_tables(L, False) + _trig_tables(L, True) + _hy_positions(L)
    consts = (ca, sa, cd, sd, gmat, hy_tabs)
    x = x3.reshape(B * L, D_MODEL)
    stacked = {name: params[name] for name in STACKED}
    for l in range(DEPTH):
        lp = {name: arr[l] for name, arr in params.items() if name not in STACKED}
        x = _encoder_layer(x, lp, stacked, l, 0.8 - 0.6 * math.exp(-0.3 * l), consts, B, L)
    return x.reshape(B, L, D_MODEL)


def kernel(x_prompt, x_sample, w_in, w_out, ln1_g, ln1_b, ln2_g, ln2_b, lam_q1, lam_k1, lam_q2,
           lam_k2, a_subln_g, hy_conv_w, hy_conv_b, hy_w1, hy_b1, hy_w2, hy_b2, hy_w3, hy_freq,
           hy_bias, na_rpb, d_sink, router_w, router_bias, e_gate, e_up, e_down, s_gate, s_up,
           s_down):
    params = {
        'w_in': w_in, 'w_out': w_out, 'ln1_g': ln1_g, 'ln1_b': ln1_b, 'ln2_g': ln2_g, 'ln2_b': ln2_b,
        'lam_q1': lam_q1, 'lam_k1': lam_k1, 'lam_q2': lam_q2, 'lam_k2': lam_k2,
        'a_subln_g': a_subln_g, 'hy_conv_w': hy_conv_w, 'hy_conv_b': hy_conv_b, 'hy_w1': hy_w1,
        'hy_b1': hy_b1, 'hy_w2': hy_w2, 'hy_b2': hy_b2, 'hy_w3': hy_w3, 'hy_freq': hy_freq,
        'hy_bias': hy_bias, 'na_rpb': na_rpb, 'd_sink': d_sink, 'router_w': router_w,
        'router_bias': router_bias, 'e_gate': e_gate, 'e_up': e_up, 'e_down': e_down,
        's_gate': s_gate, 's_up': s_up, 's_down': s_down,
    }
    return _trunk(x_prompt, params), _trunk(x_sample, params)
```
